```python
import math
import jax
import jax.numpy as jnp
from jax import lax
import numpy as np

D_MODEL = 1024
BATCH = 2
SEQ = 16384
DEPTH = 1
DEC_BATCH = 8
DEC_SEQ = 64
PAST_LEN = 1024

CHUNK = 64
ATTN_HEADS = 4
ATTN_HEAD_DIM = D_MODEL // 16
D_ATTN_QK = ATTN_HEADS * 2 * ATTN_HEAD_DIM
D_ATTN = ATTN_HEADS * 2 * ATTN_HEAD_DIM
REC_HEADS = 4
REC_KEY_DIM = D_MODEL // 8
REC_VAL_DIM = D_MODEL // 8
D_REC_K = REC_HEADS * REC_KEY_DIM
D_REC = REC_HEADS * REC_VAL_DIM
D_MIX = D_ATTN + D_REC
SPLIT_SIZES = (D_ATTN_QK, D_ATTN_QK, D_ATTN, D_ATTN, D_REC_K, D_REC_K, D_REC, D_REC)
PROJ_WIDTH = 2 * D_ATTN_QK + 2 * D_ATTN + 2 * D_REC_K + 2 * D_REC
Q_BLOCK = 128
EPS = 1e-6
ATTN_SCALE = ATTN_HEAD_DIM ** -0.5

kernel_name = 'hybrid_diffattn_hgrn2_streaming'


def _rmsnorm(x, g):
    xf = x.astype(jnp.float32)
    y = xf * lax.rsqrt(jnp.mean(xf * xf, axis=-1, keepdims=True) + EPS)
    return (y * g.astype(jnp.float32)).astype(x.dtype)


def _project(xn, w):
    u = xn @ w
    parts, start = [], 0
    for size in SPLIT_SIZES:
        parts.append(u[..., start:start + size])
        start += size
    return parts


def _attn_heads(qa, ka, va):
    b, s = qa.shape[:2]
    q = qa.reshape(b, s, ATTN_HEADS, 2, ATTN_HEAD_DIM)
    k = ka.reshape(b, s, ATTN_HEADS, 2, ATTN_HEAD_DIM)
    v = va.reshape(b, s, ATTN_HEADS, 2 * ATTN_HEAD_DIM)
    return q, k, v


def _diff_attend(q, k, v, q_pos, k_pos, lam, slopes):
    scores = jnp.einsum('bqhmd,bkhmd->bhmqk', q, k).astype(jnp.float32) * ATTN_SCALE
    dist = jnp.abs(q_pos[:, None] - k_pos[None, :]).astype(jnp.float32)
    bias = -slopes[:, None, None, None] * dist
    visible = (k_pos[None, :] // CHUNK) <= (q_pos[:, None] // CHUNK)
    scores = jnp.where(visible, scores + bias, -jnp.inf)
    p = jax.nn.softmax(scores, axis=-1)
    a = p[:, :, 0] - lam * p[:, :, 1]
    return jnp.einsum('bhqk,bkhe->bqhe', a.astype(v.dtype), v)


def _diff_attn_blocked(q, k, v, lam, slopes):
    b, s = q.shape[:2]
    nblk = s // Q_BLOCK
    qb = jnp.moveaxis(q.reshape(b, nblk, Q_BLOCK, ATTN_HEADS, 2, ATTN_HEAD_DIM), 1, 0)
    k_pos = jnp.arange(s, dtype=jnp.int32)
    q_pos = k_pos.reshape(nblk, Q_BLOCK)

    def one_block(args):
        q_blk, qp = args
        return _diff_attend(q_blk, k, v, qp, k_pos, lam, slopes)

    out = lax.map(one_block, (qb, q_pos))
    return jnp.moveaxis(out, 0, 1).reshape(b, s, ATTN_HEADS, 2 * ATTN_HEAD_DIM)


def _rec_inputs(qr, fr, ir, lb):
    b, s = qr.shape[:2]
    f32 = jnp.float32

    def heads(t, d):
        return jnp.swapaxes(t.reshape(b, s, REC_HEADS, d), 1, 2)

    fpre = fr.astype(f32)
    log_f = jnp.logaddexp(jnp.log(lb), jnp.log1p(-lb) + jax.nn.log_sigmoid(fpre))
    key = (1.0 - lb) * jax.nn.sigmoid(-fpre)
    q = jax.nn.silu(qr.astype(f32))
    return heads(q, REC_KEY_DIM), heads(key, REC_KEY_DIM), heads(log_f, REC_KEY_DIM), heads(ir.astype(f32), REC_VAL_DIM)


def _rec_chunk(state, q, k, log_f, v):
    c = q.shape[2]
    cum = jnp.cumsum(log_f, axis=2)
    causal = jnp.tril(jnp.ones((c, c), dtype=bool))
    diff = cum[:, :, :, None, :] - cum[:, :, None, :, :]
    decay = jnp.exp(jnp.where(causal[None, None, :, :, None], diff, -jnp.inf))
    scores = jnp.einsum('bhtsd,bhsd->bhts', decay * q[:, :, :, None, :], k)
    o = jnp.einsum('bhts,bhsv->bhtv', scores, v) + jnp.einsum('bhtd,bhdv->bhtv', q * jnp.exp(cum), state)
    last = cum[:, :, -1:, :]
    new_state = jnp.exp(last[:, :, 0, :, None]) * state + jnp.einsum('bhsd,bhsv->bhdv', k * jnp.exp(last - cum), v)
    return new_state, o


def _rec_scan(q, k, log_f, v):
    b, h, s, _ = q.shape
    nc = s // CHUNK

    def chunks(t):
        return jnp.moveaxis(t.reshape(b, h, nc, CHUNK, t.shape[-1]), 2, 0)

    s0 = jnp.zeros((b, h, REC_KEY_DIM, REC_VAL_DIM), jnp.float32)

    def step(state, inp):
        return _rec_chunk(state, *inp)

    s_fin, o = lax.scan(step, s0, (chunks(q), chunks(k), chunks(log_f), chunks(v)))
    return s_fin, jnp.moveaxis(o, 0, 2).reshape(b, h, s, REC_VAL_DIM)


def _merge(oa, orec, za, zr, sub_g, rec_g, lam_init, w_out_l):
    b, s = za.shape[:2]
    oa = _rmsnorm(oa, sub_g) * (1.0 - lam_init)
    orec = _rmsnorm(jnp.swapaxes(orec, 1, 2), rec_g)
    mix = jnp.concatenate([
        oa.reshape(b, s, D_ATTN) * jax.nn.silu(za),
        orec.reshape(b, s, D_REC).astype(za.dtype) * jax.nn.silu(zr)], axis=-1)
    return mix @ w_out_l


def setup_inputs(seed: int = 0) -> dict:
    key = jax.random.key(seed)
    ks = jax.random.split(key, 13)
    nrm = jax.random.normal
    f32 = jnp.float32
    return {
        'x_prompt': nrm(ks[0], (BATCH, SEQ, D_MODEL), f32),
        'x_sample': nrm(ks[1], (DEC_BATCH, DEC_SEQ, D_MODEL), f32),
        'cache_k': nrm(ks[2], (DEPTH, DEC_BATCH, PAST_LEN, ATTN_HEADS, 2 * ATTN_HEAD_DIM), f32),
        'cache_v': nrm(ks[3], (DEPTH, DEC_BATCH, PAST_LEN, ATTN_HEADS, 2 * ATTN_HEAD_DIM), f32),
        'state_h': 0.5 * nrm(ks[4], (DEPTH, DEC_BATCH, REC_HEADS, REC_KEY_DIM, REC_VAL_DIM), f32),
        'norm_g': 1.0 + 0.02 * nrm(ks[5], (DEPTH, D_MODEL), f32),
        'w_in': nrm(ks[6], (DEPTH, D_MODEL, PROJ_WIDTH), f32) * D_MODEL ** -0.5,
        'lambda_qk': 0.1 * nrm(ks[7], (DEPTH, 4, ATTN_HEAD_DIM), f32),
        'subln_g': 1.0 + 0.02 * nrm(ks[8], (DEPTH, 2 * ATTN_HEAD_DIM), f32),
        'rec_lb': 0.1 * nrm(ks[9], (DEPTH + 1, D_REC_K), f32),
        'rec_norm_g': 1.0 + 0.02 * nrm(ks[10], (DEPTH, REC_VAL_DIM), f32),
        'w_out': nrm(ks[11], (DEPTH, D_MIX, D_MODEL), f32) * D_MIX ** -0.5,
        'final_g': 1.0 + 0.02 * nrm(ks[12], (D_MODEL,), f32),
    }


def reference(x_prompt, x_sample, cache_k, cache_v, state_h, norm_g, w_in, lambda_qk, subln_g, rec_lb, rec_norm_g, w_out, final_g):
    f32 = jnp.float32
    slopes = jnp.exp2(-8.0 * jnp.arange(1, ATTN_HEADS + 1, dtype=f32) / ATTN_HEADS)
    lb_all = jnp.cumsum(jax.nn.softmax(rec_lb.astype(f32), axis=0), axis=0)
    bp, sp = x_prompt.shape[:2]
    bs, ts = x_sample.shape[:2]
    past = cache_k.shape[2]
    hp, hs = x_prompt, x_sample
    kp_l, vp_l, sp_l, ks_l, vs_l, ss_l = [], [], [], [], [], []
    for l in range(DEPTH):
        lam_init = 0.8 - 0.6 * math.exp(-0.3 * l)
        lq = lambda_qk[l].astype(f32)
        lam = jnp.exp(jnp.sum(lq[0] * lq[1])) - jnp.exp(jnp.sum(lq[2] * lq[3])) + lam_init
        lb = lb_all[l]

        qa, ka, va, za, qr, fr, ir, zr = _project(_rmsnorm(hp, norm_g[l]), w_in[l])
        q, k, v = _attn_heads(qa, ka, va)
        oa = _diff_attn_blocked(q, k, v, lam, slopes)
        rq, rk, rlf, rv = _rec_inputs(qr, fr, ir, lb)
        h_p, orec = _rec_scan(rq, rk, rlf, rv)
        hp = hp + _merge(oa, orec, za, zr, subln_g[l], rec_norm_g[l], lam_init, w_out[l])
        kp_l.append(k.reshape(bp, sp, ATTN_HEADS, 2 * ATTN_HEAD_DIM))
        vp_l.append(v)
        sp_l.append(h_p)

        qa, ka, va, za, qr, fr, ir, zr = _project(_rmsnorm(hs, norm_g[l]), w_in[l])
        q, k, v = _attn_heads(qa, ka, va)
        k_all = jnp.concatenate([cache_k[l].reshape(bs, past, ATTN_HEADS, 2, ATTN_HEAD_DIM).astype(k.dtype), k], axis=1)
        v_all = jnp.concatenate([cache_v[l].astype(v.dtype), v], axis=1)
        q_pos = past + jnp.arange(ts, dtype=jnp.int32)
        k_pos = jnp.arange(past + ts, dtype=jnp.int32)
        oa = _diff_attend(q, k_all, v_all, q_pos, k_pos, lam, slopes)
        rq, rk, rlf, rv = _rec_inputs(qr, fr, ir, lb)
        h_s, orec = _rec_chunk(state_h[l].astype(f32), rq, rk, rlf, rv)
        hs = hs + _merge(oa, orec, za, zr, subln_g[l], rec_norm_g[l], lam_init, w_out[l])
        ks_l.append(k.reshape(bs, ts, ATTN_HEADS, 2 * ATTN_HEAD_DIM))
        vs_l.append(v)
        ss_l.append(h_s)

    y_prompt = _rmsnorm(hp, final_g)
    y_sample = _rmsnorm(hs, final_g)
    new_k_prompt = jnp.stack(kp_l, axis=0)
    new_v_prompt = jnp.stack(vp_l, axis=0)
    new_h_prompt = jnp.stack(sp_l, axis=0)
    new_k_sample = jnp.stack(ks_l, axis=0)
    new_v_sample = jnp.stack(vs_l, axis=0)
    new_h_sample = jnp.stack(ss_l, axis=0)
    return (y_prompt, y_sample, new_k_prompt, new_v_prompt, new_h_prompt, new_k_sample, new_v_sample, new_h_sample)
```

```python
import functools
import math

import jax
import jax.numpy as jnp
from jax import lax
from jax.experimental import pallas as pl
from jax.experimental.pallas import tpu as pltpu

F32 = jnp.float32
BF16 = jnp.bfloat16

D_MODEL = 1024
CHUNK = 64
ATTN_HEADS = 4
ATTN_HEAD_DIM = 64
ATTN_VAL_DIM = 2 * ATTN_HEAD_DIM
REC_HEADS = 4
REC_DIM = 128
GROUP_WIDTH = 512
N_GROUPS = 8
EPS = 1e-6
ATTN_SCALE = ATTN_HEAD_DIM ** -0.5
NEG_BIG = -1e30

VMEM_LIMIT_BYTES = 52 * 1024 * 1024
ATTN_TILE = 256
REC_TILE = 512
LEVEL_HALVES = (32, 16, 8, 4, 2, 1)


def _row_tile(seq, want):
    return want if seq % want == 0 else seq


def _cparams(sem):
    return pltpu.CompilerParams(dimension_semantics=sem, vmem_limit_bytes=VMEM_LIMIT_BYTES)


def _sigmoid(x):
    return 1.0 / (1.0 + jnp.exp(-x))


def _proj_kernel(x_ref, g_ref, w_ref, lb_ref,
                 qT_ref, kb_ref, vT_ref, k32_ref, v32_ref, ga_ref,
                 rq_ref, rk_ref, rlf_ref, rv_ref, gr_ref):
    x = x_ref[0]
    ms = jnp.mean(x * x, axis=-1, keepdims=True)
    xn = (x * lax.rsqrt(ms + EPS) * g_ref[...]).astype(BF16)

    def proj(c):
        return jnp.dot(xn, w_ref[:, c * GROUP_WIDTH:(c + 1) * GROUP_WIDTH],
                       preferred_element_type=F32)

    qa = proj(0) * ATTN_SCALE
    for h in range(ATTN_HEADS):
        qT_ref[0, h] = qa[:, h * 128:(h + 1) * 128].T.astype(BF16)
    ka = proj(1)
    k32_ref[0] = ka
    kb_ref[0] = ka.astype(BF16)
    va = proj(2)
    v32_ref[0] = va
    for h in range(ATTN_HEADS):
        vT_ref[0, h] = va[:, h * 128:(h + 1) * 128].T.astype(BF16)
    za = proj(3)
    ga_ref[0] = za * _sigmoid(za)
    qr = proj(4)
    rq_ref[0] = qr * _sigmoid(qr)
    r0 = lb_ref[0:1, :]
    r1 = lb_ref[1:2, :]
    rmax = jnp.maximum(r0, r1)
    e0 = jnp.exp(r0 - rmax)
    e1 = jnp.exp(r1 - rmax)
    lb = e0 / (e0 + e1)
    fr = proj(5)
    key = (1.0 - lb) * _sigmoid(-fr)
    rk_ref[0] = key
    rlf_ref[0] = jnp.log1p(-key)
    rv_ref[0] = proj(6)
    zr = proj(7)
    gr_ref[0] = zr * _sigmoid(zr)


def _project(x, norm_g, w_bf16, rec_lb, tm):
    b, s, d = x.shape
    nt = s // tm
    row = lambda bi, ti: (bi, ti, 0)
    rowT = lambda bi, ti: (bi, 0, 0, ti)
    f32_rows = jax.ShapeDtypeStruct((b, s, GROUP_WIDTH), F32)
    bf_rows = jax.ShapeDtypeStruct((b, s, GROUP_WIDTH), BF16)
    bf_T = jax.ShapeDtypeStruct((b, ATTN_HEADS, 128, s), BF16)
    rows_spec = pl.BlockSpec((1, tm, GROUP_WIDTH), row)
    T_spec = pl.BlockSpec((1, ATTN_HEADS, 128, tm), rowT)
    return pl.pallas_call(
        _proj_kernel,
        grid=(b, nt),
        in_specs=[
            pl.BlockSpec((1, tm, d), row),
            pl.BlockSpec((1, d), lambda bi, ti: (0, 0)),
            pl.BlockSpec((d, N_GROUPS * GROUP_WIDTH), lambda bi, ti: (0, 0)),
            pl.BlockSpec((2, GROUP_WIDTH), lambda bi, ti: (0, 0)),
        ],
        out_specs=[T_spec, rows_spec, T_spec, rows_spec, rows_spec, rows_spec,
                   rows_spec, rows_spec, rows_spec, rows_spec, rows_spec],
        out_shape=[bf_T, bf_rows, bf_T, f32_rows, f32_rows, f32_rows,
                   f32_rows, f32_rows, f32_rows, f32_rows, f32_rows],
        compiler_params=_cparams(("parallel", "parallel")),
        name="proj",
    )(x, norm_g, w_bf16, rec_lb)


def _lambda_full(lq_ref, lam_init):
    lq = lq_ref[...]
    s01 = jnp.sum(lq[0:1] * lq[1:2], axis=-1, keepdims=True)
    s23 = jnp.sum(lq[2:3] * lq[3:4], axis=-1, keepdims=True)
    return jnp.exp(s01) - jnp.exp(s23) + lam_init


def _attn_kernel(slopes_ref, lq_ref, qT_ref, k_ref, vT_ref, g_ref, gate_ref,
                 o_ref,
                 qt_s, boff_s, bdiag_s, m_s, l_s, acc_s, *, tile, lam_init):
    T = tile
    h = pl.program_id(1)
    qi = pl.program_id(2)
    slope = slopes_ref[h]

    @pl.when(qi == 0)
    def _():
        jl = lax.broadcasted_iota(jnp.int32, (T, 2 * T), 0)
        il = lax.broadcasted_iota(jnp.int32, (T, 2 * T), 1) & (T - 1)
        before = slope * (jl - T).astype(F32)
        after = slope * (2 * il - jl - T).astype(F32)
        boff_s[...] = before
        visible = (jl // CHUNK) <= (il // CHUNK)
        bdiag_s[...] = jnp.where(visible, jnp.where(jl <= il, before, after), NEG_BIG)

    qt = qT_ref[0, 0]
    row = lax.broadcasted_iota(jnp.int32, qt.shape, 0)
    zero = jnp.zeros_like(qt)
    qt_s[:, :T] = jnp.where(row < ATTN_HEAD_DIM, qt, zero)
    qt_s[:, T:] = jnp.where(row >= ATTN_HEAD_DIM, qt, zero)
    m_s[...] = jnp.full(m_s.shape, NEG_BIG, F32)
    l_s[...] = jnp.zeros(l_s.shape, F32)
    acc_s[...] = jnp.zeros(acc_s.shape, F32)

    def block(j, bias_ref):
        start = pl.multiple_of(j * T, T)
        kt = k_ref[0, pl.ds(start, T), :]
        s = jnp.dot(kt, qt_s[...], preferred_element_type=F32)
        t = s + bias_ref[...]
        m_prev = m_s[...] - slope * T
        m_new = jnp.maximum(m_prev, jnp.max(t, axis=0, keepdims=True))
        alpha = jnp.exp(m_prev - m_new)
        p = jnp.exp(t - m_new)
        l_s[...] = alpha * l_s[...] + jnp.sum(p, axis=0, keepdims=True)
        vt = vT_ref[0, 0, :, pl.ds(start, T)]
        acc_s[...] = alpha * acc_s[...] + jnp.dot(vt, p.astype(BF16), preferred_element_type=F32)
        m_s[...] = m_new

    def body(j, carry):
        block(j, boff_s)
        return carry

    lax.fori_loop(0, qi, body, 0)
    block(qi, bdiag_s)

    lam = _lambda_full(lq_ref, lam_init)
    accn = acc_s[...] / l_s[...]
    oT = accn[:, :T] - lam * accn[:, T:]
    ms = jnp.mean(oT * oT, axis=0, keepdims=True)
    on = oT * lax.rsqrt(ms + EPS) * g_ref[...] * (1.0 - lam_init)
    o_ref[0] = on.T * gate_ref[0]


def _attention_prompt(slopes, lq, qT, kb, vT, sub_g_col, gate, lam_init):
    b, heads, _, s = qT.shape
    T = ATTN_TILE
    assert s % T == 0 and T % CHUNK == 0
    nq = s // T
    kern = functools.partial(_attn_kernel, tile=T, lam_init=lam_init)
    return pl.pallas_call(
        kern,
        grid=(b, heads, nq),
        in_specs=[
            pl.BlockSpec(memory_space=pltpu.SMEM),
            pl.BlockSpec((4, ATTN_HEAD_DIM), lambda bi, hi, qi: (0, 0)),
            pl.BlockSpec((1, 1, 128, T), lambda bi, hi, qi: (bi, hi, 0, qi)),
            pl.BlockSpec((1, s, 128), lambda bi, hi, qi: (bi, 0, hi)),
            pl.BlockSpec((1, 1, 128, s), lambda bi, hi, qi: (bi, hi, 0, 0)),
            pl.BlockSpec((128, 1), lambda bi, hi, qi: (0, 0)),
            pl.BlockSpec((1, T, 128), lambda bi, hi, qi: (bi, qi, hi)),
        ],
        out_specs=pl.BlockSpec((1, T, 128), lambda bi, hi, qi: (bi, qi, hi)),
        out_shape=jax.ShapeDtypeStruct((b, s, heads * 128), F32),
        scratch_shapes=[
            pltpu.VMEM((128, 2 * T), BF16),
            pltpu.VMEM((T, 2 * T), F32),
            pltpu.VMEM((T, 2 * T), F32),
            pltpu.VMEM((1, 2 * T), F32),
            pltpu.VMEM((1, 2 * T), F32),
            pltpu.VMEM((128, 2 * T), F32),
        ],
        compiler_params=_cparams(("parallel", "parallel", "arbitrary")),
        name="attn_prompt",
    )(slopes, lq, qT, kb, vT, sub_g_col, gate)


def _attn_sample_kernel(slopes_ref, lq_ref, qT_ref, ck_ref, kn_ref, cv_ref, vn_ref, g_ref, gate_ref,
                        o_ref, *, past, lam_init):
    h = pl.program_id(1)
    slope = slopes_ref[h]
    ts = kn_ref.shape[1]
    k_all = jnp.concatenate([ck_ref[0].astype(BF16), kn_ref[0]], axis=0)
    v_all = jnp.concatenate([cv_ref[0].astype(BF16), vn_ref[0].astype(BF16)], axis=0)
    qt = qT_ref[0, 0]
    row = lax.broadcasted_iota(jnp.int32, qt.shape, 0)
    zero = jnp.zeros_like(qt)
    kpos = lax.broadcasted_iota(jnp.int32, (past + ts, ts), 0)
    qpos = past + lax.broadcasted_iota(jnp.int32, (past + ts, ts), 1)
    bias = -slope * jnp.abs(qpos - kpos).astype(F32)
    visible = (kpos // CHUNK) <= (qpos // CHUNK)

    def softmax_t(q_masked):
        s = jnp.dot(k_all, q_masked, preferred_element_type=F32)
        t = jnp.where(visible, s + bias, NEG_BIG)
        p = jnp.exp(t - jnp.max(t, axis=0, keepdims=True))
        return p / jnp.sum(p, axis=0, keepdims=True)

    p1 = softmax_t(jnp.where(row < ATTN_HEAD_DIM, qt, zero))
    p2 = softmax_t(jnp.where(row >= ATTN_HEAD_DIM, qt, zero))
    lam = _lambda_full(lq_ref, lam_init)
    a = (p1 - lam * p2).astype(BF16)
    o = lax.dot_general(a, v_all, (((0,), (0,)), ((), ())), preferred_element_type=F32)
    ms = jnp.mean(o * o, axis=-1, keepdims=True)
    on = o * lax.rsqrt(ms + EPS) * g_ref[...] * (1.0 - lam_init)
    o_ref[0] = on * gate_ref[0]


def _attention_sample(slopes, lq, qT, cache_k, kb, cache_v, v32, sub_g_row, gate, lam_init):
    b, heads, _, ts = qT.shape
    past = cache_k.shape[1]
    kern = functools.partial(_attn_sample_kernel, past=past, lam_init=lam_init)
    col = lambda bi, hi: (bi, 0, hi)
    return pl.pallas_call(
        kern,
        grid=(b, heads),
        in_specs=[
            pl.BlockSpec(memory_space=pltpu.SMEM),
            pl.BlockSpec((4, ATTN_HEAD_DIM), lambda bi, hi: (0, 0)),
            pl.BlockSpec((1, 1, 128, ts), lambda bi, hi: (bi, hi, 0, 0)),
            pl.BlockSpec((1, past, 128), col),
            pl.BlockSpec((1, ts, 128), col),
            pl.BlockSpec((1, past, 128), col),
            pl.BlockSpec((1, ts, 128), col),
            pl.BlockSpec((1, 128), lambda bi, hi: (0, 0)),
            pl.BlockSpec((1, ts, 128), col),
        ],
        out_specs=pl.BlockSpec((1, ts, 128), col),
        out_shape=jax.ShapeDtypeStruct((b, ts, heads * 128), F32),
        compiler_params=_cparams(("parallel", "parallel")),
        name="attn_sample",
    )(slopes, lq, qT, cache_k, kb, cache_v, v32, sub_g_row, gate)


def _rec_prep_kernel(rq_ref, rk_ref, rlf_ref, rv_ref,
                     oi_ref, qe_ref, kd_ref, el_ref):
    C = CHUNK
    n_chunks = rq_ref.shape[1] // C
    t_idx = lax.broadcasted_iota(jnp.int32, (C, C), 0)
    s_idx = lax.broadcasted_iota(jnp.int32, (C, C), 1)
    tri = (s_idx <= t_idx).astype(F32)
    masks = []
    sels = []
    for n in LEVEL_HALVES:
        same = (t_idx // (2 * n)) == (s_idx // (2 * n))
        masks.append(same & ((t_idx % (2 * n)) >= n) & ((s_idx % (2 * n)) < n))
        sels.append((s_idx == (t_idx // (2 * n)) * (2 * n) + n - 1).astype(F32))
    diag = t_idx == s_idx
    sel_all = jnp.concatenate(sels, axis=0)

    def chunk(c, carry):
        r0 = pl.multiple_of(c * C, C)
        rows = pl.ds(r0, C)
        q = rq_ref[0, rows, :]
        k = rk_ref[0, rows, :]
        v = rv_ref[0, rows, :].astype(BF16)
        lf = rlf_ref[0, rows, :]
        cum = jnp.dot(tri, lf, precision=lax.Precision.HIGHEST, preferred_element_type=F32)
        last = cum[C - 1:C, :]
        qe_ref[0, rows, :] = (q * jnp.exp(cum)).astype(BF16)
        kd_ref[0, rows, :] = (k * jnp.exp(last - cum)).astype(BF16)
        el_ref[0, pl.ds(c, 1), :] = jnp.exp(last)
        ref_all = jnp.dot(sel_all, cum, precision=lax.Precision.HIGHEST, preferred_element_type=F32)
        q_lv = [q.astype(BF16)]
        k_lv = [k.astype(BF16)]
        for i in range(len(LEVEL_HALVES)):
            ref = ref_all[i * C:(i + 1) * C, :]
            q_lv.append((q * jnp.exp(jnp.minimum(cum - ref, 0.0))).astype(BF16))
            k_lv.append((k * jnp.exp(jnp.minimum(ref - cum, 0.0))).astype(BF16))
        level_masks = [diag] + masks
        for h in range(REC_HEADS):
            cols = slice(h * REC_DIM, (h + 1) * REC_DIM)
            a = jnp.zeros((C, C), F32)
            for ql, kl, mk in zip(q_lv, k_lv, level_masks):
                x = lax.dot_general(ql[:, cols], kl[:, cols], (((1,), (1,)), ((), ())),
                                    preferred_element_type=F32)
                a = jnp.where(mk, x, a)
            oi_ref[0, rows, cols] = jnp.dot(a.astype(BF16), v[:, cols], preferred_element_type=F32)
        return carry

    lax.fori_loop(0, n_chunks, chunk, 0)


def _rec_prep(rq, rk, rlf, rv, tr):
    b, s, w = rq.shape
    nt = s // tr
    ncs = tr // CHUNK
    row = lambda bi, ti: (bi, ti, 0)
    rows_spec = pl.BlockSpec((1, tr, w), row)
    return pl.pallas_call(
        _rec_prep_kernel,
        grid=(b, nt),
        in_specs=[rows_spec] * 4,
        out_specs=[rows_spec, rows_spec, rows_spec, pl.BlockSpec((1, ncs, w), row)],
        out_shape=[jax.ShapeDtypeStruct((b, s, w), F32),
                   jax.ShapeDtypeStruct((b, s, w), BF16),
                   jax.ShapeDtypeStruct((b, s, w), BF16),
                   jax.ShapeDtypeStruct((b, s // CHUNK, w), F32)],
        compiler_params=_cparams(("parallel", "parallel")),
        name="rec_prep",
    )(rq, rk, rlf, rv)


def _rec_seq_kernel(oi_ref, qe_ref, kd_ref, rv_ref, el_ref, gate_ref, h0_ref, g_ref,
                    o_ref, h_ref, st_s):
    C = CHUNK
    ti = pl.program_id(1)
    n_chunks = oi_ref.shape[1] // C

    @pl.when(ti == 0)
    def _():
        for h in range(REC_HEADS):
            st_s[h] = h0_ref[0, h].T

    for c in range(n_chunks):
        rows = slice(c * C, (c + 1) * C)
        for h in range(REC_HEADS):
            cols = slice(h * REC_DIM, (h + 1) * REC_DIM)
            st = st_s[h]
            o = oi_ref[0, rows, cols] + lax.dot_general(
                qe_ref[0, rows, cols], st.astype(BF16), (((1,), (1,)), ((), ())),
                preferred_element_type=F32)
            upd = lax.dot_general(rv_ref[0, rows, cols].astype(BF16), kd_ref[0, rows, cols],
                                  (((0,), (0,)), ((), ())), preferred_element_type=F32)
            st_s[h] = st * el_ref[0, c:c + 1, cols] + upd
            ms = jnp.mean(o * o, axis=-1, keepdims=True)
            on = o * lax.rsqrt(ms + EPS) * g_ref[...]
            o_ref[0, rows, cols] = on * gate_ref[0, rows, cols]

    @pl.when(ti == pl.num_programs(1) - 1)
    def _():
        for h in range(REC_HEADS):
            h_ref[0, h] = st_s[h].T


def _rec_seq(oi, qe, kd, rv, el, gate, h0, rec_g_row, tr):
    b, s, w = oi.shape
    nt = s // tr
    ncs = tr // CHUNK
    row = lambda bi, ti: (bi, ti, 0)
    rows_spec = pl.BlockSpec((1, tr, w), row)
    st_spec = pl.BlockSpec((1, REC_HEADS, REC_DIM, REC_DIM), lambda bi, ti: (bi, 0, 0, 0))
    return pl.pallas_call(
        _rec_seq_kernel,
        grid=(b, nt),
        in_specs=[rows_spec, rows_spec, rows_spec, rows_spec,
                  pl.BlockSpec((1, ncs, w), row), rows_spec, st_spec,
                  pl.BlockSpec((1, REC_DIM), lambda bi, ti: (0, 0))],
        out_specs=[rows_spec, st_spec],
        out_shape=[jax.ShapeDtypeStruct((b, s, w), F32),
                   jax.ShapeDtypeStruct((b, REC_HEADS, REC_DIM, REC_DIM), F32)],
        scratch_shapes=[pltpu.VMEM((REC_HEADS, REC_DIM, REC_DIM), F32)],
        compiler_params=_cparams(("parallel", "arbitrary")),
        name="rec_seq",
    )(oi, qe, kd, rv, el, gate, h0, rec_g_row)


def _merge_kernel(x_ref, ma_ref, mr_ref, w_ref, g_ref, y_ref):
    half = ma_ref.shape[2]
    y = x_ref[0]
    y = y + jnp.dot(ma_ref[0].astype(BF16), w_ref[:half, :], preferred_element_type=F32)
    y = y + jnp.dot(mr_ref[0].astype(BF16), w_ref[half:, :], preferred_element_type=F32)
    ms = jnp.mean(y * y, axis=-1, keepdims=True)
    y_ref[0] = y * lax.rsqrt(ms + EPS) * g_ref[...]


def _merge(x, mix_a, mix_r, w_out_bf16, final_g, tm):
    b, s, d = x.shape
    nt = s // tm
    row = lambda bi, ti: (bi, ti, 0)
    return pl.pallas_call(
        _merge_kernel,
        grid=(b, nt),
        in_specs=[
            pl.BlockSpec((1, tm, d), row),
            pl.BlockSpec((1, tm, mix_a.shape[2]), row),
            pl.BlockSpec((1, tm, mix_r.shape[2]), row),
            pl.BlockSpec(w_out_bf16.shape, lambda bi, ti: (0, 0)),
            pl.BlockSpec((1, d), lambda bi, ti: (0, 0)),
        ],
        out_specs=pl.BlockSpec((1, tm, d), row),
        out_shape=jax.ShapeDtypeStruct((b, s, d), F32),
        compiler_params=_cparams(("parallel", "parallel")),
        name="merge",
    )(x, mix_a, mix_r, w_out_bf16, final_g)


def kernel(x_prompt, x_sample, cache_k, cache_v, state_h, norm_g, w_in, lambda_qk, subln_g, rec_lb,
           rec_norm_g, w_out, final_g):
    depth = w_in.shape[0]
    assert depth == 1
    l = 0
    lam_init = 0.8 - 0.6 * math.exp(-0.3 * l)
    slopes = jnp.exp2(-8.0 * jnp.arange(1, ATTN_HEADS + 1, dtype=F32) / ATTN_HEADS)
    bp, sp, _ = x_prompt.shape
    bs, ts, _ = x_sample.shape
    past = cache_k.shape[2]

    w_in_b = w_in[l].astype(BF16)
    w_out_b = w_out[l].astype(BF16)
    g_in = norm_g[l].reshape(1, D_MODEL)
    lq = lambda_qk[l]
    sub_g = subln_g[l]
    rec_g = rec_norm_g[l].reshape(1, REC_DIM)
    fin_g = final_g.reshape(1, D_MODEL)
    lb_rows = rec_lb[l:l + 2]

    tm = _row_tile(sp, 256)
    (qT, kb, vT, k32, v32, ga, rq, rk, rlf, rv, gr) = _project(x_prompt, g_in, w_in_b, lb_rows, tm)
    mix_a = _attention_prompt(slopes, lq, qT, kb, vT, sub_g.reshape(128, 1), ga, lam_init)
    tr = _row_tile(sp, REC_TILE)
    oi, qe, kd, el = _rec_prep(rq, rk, rlf, rv, tr)
    h0 = jnp.zeros((bp, REC_HEADS, REC_DIM, REC_DIM), F32)
    mix_r, h_p = _rec_seq(oi, qe, kd, rv, el, gr, h0, rec_g, tr)
    y_prompt = _merge(x_prompt, mix_a, mix_r, w_out_b, fin_g, _row_tile(sp, 512))

    (qT_s, kb_s, _, k32_s, v32_s, ga_s, rq_s, rk_s, rlf_s, rv_s, gr_s) = _project(
        x_sample, g_in, w_in_b, lb_rows, ts)
    ck = cache_k[l].reshape(bs, past, ATTN_HEADS * 128)
    cv = cache_v[l].reshape(bs, past, ATTN_HEADS * 128)
    mix_a_s = _attention_sample(slopes, lq, qT_s, ck, kb_s, cv, v32_s, sub_g.reshape(1, 128), ga_s, lam_init)
    oi_s, qe_s, kd_s, el_s = _rec_prep(rq_s, rk_s, rlf_s, rv_s, ts)
    mix_r_s, h_s = _rec_seq(oi_s, qe_s, kd_s, rv_s, el_s, gr_s, state_h[l].astype(F32), rec_g, ts)
    y_sample = _merge(x_sample, mix_a_s, mix_r_s, w_out_b, fin_g, ts)

    shape_kv = lambda a, b_, s_: a.reshape(1, b_, s_, ATTN_HEADS, ATTN_VAL_DIM)
    return (y_prompt, y_sample,
            shape_kv(k32, bp, sp), shape_kv(v32, bp, sp), h_p[None],
            shape_kv(k32_s, bs, ts), shape_kv(v32_s, bs, ts), h_s[None])
```

```python
import functools
import math

import jax
import jax.numpy as jnp
from jax import lax
from jax.experimental import pallas as pl
from jax.experimental.pallas import tpu as pltpu

F32 = jnp.float32
BF16 = jnp.bfloat16

D_MODEL = 1024
CHUNK = 64
ATTN_HEADS = 4
ATTN_HEAD_DIM = 64
ATTN_VAL_DIM = 2 * ATTN_HEAD_DIM
REC_HEADS = 4
REC_DIM = 128
GROUP_WIDTH = 512
N_GROUPS = 8
EPS = 1e-6
LOG2E = math.log2(math.e)
ATTN_SCALE = ATTN_HEAD_DIM ** -0.5
NEG_BIG = -1e30
M_INIT = -1e20

VMEM_LIMIT_BYTES = 52 * 1024 * 1024
ATTN_TILE = 512
ATTN_COLS = 256
ATTN_ROWS = 128
REC_TILE = 512
LEVEL_HALVES = (32, 16, 8, 4, 2, 1)


def _row_tile(seq, want):
    return want if seq % want == 0 else seq


def _cparams(sem):
    return pltpu.CompilerParams(dimension_semantics=sem, vmem_limit_bytes=VMEM_LIMIT_BYTES)


def _sigmoid(x):
    return 1.0 / (1.0 + jnp.exp(-x))


def _proj_kernel(x_ref, g_ref, w_ref, lb_ref,
                 qT_ref, kb_ref, vT_ref, k32_ref, v32_ref, ga_ref,
                 rq_ref, rk_ref, rlf_ref, rv_ref, gr_ref):
    x = x_ref[0]
    ms = jnp.mean(x * x, axis=-1, keepdims=True)
    xn = (x * lax.rsqrt(ms + EPS) * g_ref[...]).astype(BF16)

    def proj(c):
        return jnp.dot(xn, w_ref[:, c * GROUP_WIDTH:(c + 1) * GROUP_WIDTH],
                       preferred_element_type=F32)

    qa = proj(0) * (ATTN_SCALE * LOG2E)
    for h in range(ATTN_HEADS):
        qT_ref[0, h] = qa[:, h * 128:(h + 1) * 128].T.astype(BF16)
    ka = proj(1)
    k32_ref[0] = ka
    kb_ref[0] = ka.astype(BF16)
    va = proj(2)
    v32_ref[0] = va
    for h in range(ATTN_HEADS):
        vT_ref[0, h] = va[:, h * 128:(h + 1) * 128].T.astype(BF16)
    za = proj(3)
    ga_ref[0] = za * _sigmoid(za)
    qr = proj(4)
    rq_ref[0] = qr * _sigmoid(qr)
    r0 = lb_ref[0:1, :]
    r1 = lb_ref[1:2, :]
    rmax = jnp.maximum(r0, r1)
    e0 = jnp.exp(r0 - rmax)
    e1 = jnp.exp(r1 - rmax)
    lb = e0 / (e0 + e1)
    fr = proj(5)
    key = (1.0 - lb) * _sigmoid(-fr)
    rk_ref[0] = key
    rlf_ref[0] = jnp.log1p(-key)
    rv_ref[0] = proj(6)
    zr = proj(7)
    gr_ref[0] = zr * _sigmoid(zr)


def _project(x, norm_g, w_bf16, rec_lb, tm):
    b, s, d = x.shape
    nt = s // tm
    row = lambda bi, ti: (bi, ti, 0)
    rowT = lambda bi, ti: (bi, 0, 0, ti)
    f32_rows = jax.ShapeDtypeStruct((b, s, GROUP_WIDTH), F32)
    bf_rows = jax.ShapeDtypeStruct((b, s, GROUP_WIDTH), BF16)
    bf_T = jax.ShapeDtypeStruct((b, ATTN_HEADS, 128, s), BF16)
    rows_spec = pl.BlockSpec((1, tm, GROUP_WIDTH), row)
    T_spec = pl.BlockSpec((1, ATTN_HEADS, 128, tm), rowT)
    return pl.pallas_call(
        _proj_kernel,
        grid=(b, nt),
        in_specs=[
            pl.BlockSpec((1, tm, d), row),
            pl.BlockSpec((1, d), lambda bi, ti: (0, 0)),
            pl.BlockSpec((d, N_GROUPS * GROUP_WIDTH), lambda bi, ti: (0, 0)),
            pl.BlockSpec((2, GROUP_WIDTH), lambda bi, ti: (0, 0)),
        ],
        out_specs=[T_spec, rows_spec, T_spec, rows_spec, rows_spec, rows_spec,
                   rows_spec, rows_spec, rows_spec, rows_spec, rows_spec],
        out_shape=[bf_T, bf_rows, bf_T, f32_rows, f32_rows, f32_rows,
                   f32_rows, f32_rows, f32_rows, f32_rows, f32_rows],
        compiler_params=_cparams(("parallel", "parallel")),
        name="proj",
    )(x, norm_g, w_bf16, rec_lb)


def _lambda_full(lq_ref, lam_init):
    lq = lq_ref[...]
    s01 = jnp.sum(lq[0:1] * lq[1:2], axis=-1, keepdims=True)
    s23 = jnp.sum(lq[2:3] * lq[3:4], axis=-1, keepdims=True)
    return jnp.exp(s01) - jnp.exp(s23) + lam_init


def _attn_kernel(slopes_ref, lq_ref, qT_ref, k_ref, vT_ref, g_ref, gate_ref,
                 o_ref,
                 qt_s, bias_s, s_a, s_b, p_a, p_b, al_a, al_b, m_s, l_s, acc_s, *, tile, lam_init):
    T = tile
    W = 2 * T
    CW = ATTN_COLS
    RC = ATTN_ROWS
    col_chunks = [slice(c * CW, (c + 1) * CW) for c in range(W // CW)]
    row_chunks = [slice(r * RC, (r + 1) * RC) for r in range(T // RC)]
    h = pl.program_id(1)
    qi = pl.program_id(2)
    slope = slopes_ref[h] * LOG2E
    shift = slope * T

    @pl.when(qi == 0)
    def _():
        for c, cols in enumerate(col_chunks):
            jl = lax.broadcasted_iota(jnp.int32, (T, CW), 0)
            il = (lax.broadcasted_iota(jnp.int32, (T, CW), 1) + c * CW) & (T - 1)
            before = slope * (jl - T).astype(F32)
            after = slope * (2 * il - jl - T).astype(F32)
            visible = (jl // CHUNK) <= (il // CHUNK)
            bias_s[0, :, cols] = before
            bias_s[1, :, cols] = jnp.where(visible, jnp.where(jl <= il, before, after), NEG_BIG)
            bias_s[2, :, cols] = jnp.full((T, CW), NEG_BIG, F32)

    qt = qT_ref[0, 0]
    row = lax.broadcasted_iota(jnp.int32, qt.shape, 0)
    zero = jnp.zeros_like(qt)
    qt_s[:, :T] = jnp.where(row < ATTN_HEAD_DIM, qt, zero)
    qt_s[:, T:] = jnp.where(row >= ATTN_HEAD_DIM, qt, zero)
    m_s[...] = jnp.full(m_s.shape, M_INIT, F32)
    l_s[...] = jnp.zeros(l_s.shape, F32)
    acc_s[...] = jnp.zeros(acc_s.shape, F32)
    p_b[...] = jnp.zeros(p_b.shape, BF16)
    al_b[...] = jnp.ones(al_b.shape, F32)

    def stage_s(i, s_ref):
        blk = jnp.minimum(i, qi)
        bidx = jnp.clip(i - qi + 1, 0, 2)
        start = blk * T
        for cols in col_chunks:
            w = qt_s[:, cols]
            for rows in row_chunks:
                kt = k_ref[0, pl.ds(pl.multiple_of(start + rows.start, RC), RC), :]
                s_ref[rows, cols] = jnp.dot(kt, w, preferred_element_type=F32) + bias_s[bidx, rows, cols]

    def fold8(x, op):
        return op(x.reshape(RC // 8, 8, CW), axis=0)

    def stage_x(s_ref, p_ref, al_ref):
        for cols in col_chunks:
            m_prev = m_s[:, cols] - shift
            mx = fold8(s_ref[row_chunks[0], cols], jnp.max)
            for rows in row_chunks[1:]:
                mx = jnp.maximum(mx, fold8(s_ref[rows, cols], jnp.max))
            m_new = jnp.maximum(m_prev, jnp.max(mx, axis=0, keepdims=True))
            alpha = jnp.exp2(m_prev - m_new)
            lsum = jnp.zeros((8, CW), F32)
            for rows in row_chunks:
                p = jnp.exp2(s_ref[rows, cols] - m_new)
                lsum = lsum + fold8(p, jnp.sum)
                p_ref[rows, cols] = p.astype(BF16)
            l_s[:, cols] = alpha * l_s[:, cols] + jnp.sum(lsum, axis=0, keepdims=True)
            al_ref[:, cols] = alpha
            m_s[:, cols] = m_new

    def stage_p(i, p_ref, al_ref):
        blk = jnp.clip(i, 0, qi)
        vt = vT_ref[0, 0, :, pl.ds(pl.multiple_of(blk * T, T), T)]
        for cols in col_chunks:
            acc_s[:, cols] = al_ref[:, cols] * acc_s[:, cols] + jnp.dot(
                vt, p_ref[:, cols], preferred_element_type=F32)

    n_pairs = (qi + 2) // 2

    stage_s(0, s_a)

    def body(u, carry):
        i = 2 * u
        stage_x(s_a, p_a, al_a)
        stage_s(i + 1, s_b)
        stage_p(i - 1, p_b, al_b)
        stage_x(s_b, p_b, al_b)
        stage_s(i + 2, s_a)
        stage_p(i, p_a, al_a)
        return carry

    lax.fori_loop(0, n_pairs, body, 0)
    stage_p(2 * n_pairs - 1, p_b, al_b)

    lam = _lambda_full(lq_ref, lam_init)
    accn = acc_s[...] / l_s[...]
    oT = accn[:, :T] - lam * accn[:, T:]
    ms = jnp.mean(oT * oT, axis=0, keepdims=True)
    on = oT * lax.rsqrt(ms + EPS) * g_ref[...] * (1.0 - lam_init)
    o_ref[0] = on.T * gate_ref[0]


def _attention_prompt(slopes, lq, qT, kb, vT, sub_g_col, gate, lam_init):
    b, heads, _, s = qT.shape
    T = _row_tile(s, ATTN_TILE)
    assert T % ATTN_COLS == 0 and T % CHUNK == 0 and (T & (T - 1)) == 0
    nq = s // T
    kern = functools.partial(_attn_kernel, tile=T, lam_init=lam_init)
    return pl.pallas_call(
        kern,
        grid=(b, heads, nq),
        in_specs=[
            pl.BlockSpec(memory_space=pltpu.SMEM),
            pl.BlockSpec((4, ATTN_HEAD_DIM), lambda bi, hi, qi: (0, 0)),
            pl.BlockSpec((1, 1, 128, T), lambda bi, hi, qi: (bi, hi, 0, qi)),
            pl.BlockSpec((1, s, 128), lambda bi, hi, qi: (bi, 0, hi)),
            pl.BlockSpec((1, 1, 128, s), lambda bi, hi, qi: (bi, hi, 0, 0)),
            pl.BlockSpec((128, 1), lambda bi, hi, qi: (0, 0)),
            pl.BlockSpec((1, T, 128), lambda bi, hi, qi: (bi, qi, hi)),
        ],
        out_specs=pl.BlockSpec((1, T, 128), lambda bi, hi, qi: (bi, qi, hi)),
        out_shape=jax.ShapeDtypeStruct((b, s, heads * 128), F32),
        scratch_shapes=[
            pltpu.VMEM((128, 2 * T), BF16),
            pltpu.VMEM((3, T, 2 * T), F32),
            pltpu.VMEM((T, 2 * T), F32),
            pltpu.VMEM((T, 2 * T), F32),
            pltpu.VMEM((T, 2 * T), BF16),
            pltpu.VMEM((T, 2 * T), BF16),
            pltpu.VMEM((1, 2 * T), F32),
            pltpu.VMEM((1, 2 * T), F32),
            pltpu.VMEM((1, 2 * T), F32),
            pltpu.VMEM((1, 2 * T), F32),
            pltpu.VMEM((128, 2 * T), F32),
        ],
        compiler_params=_cparams(("parallel", "parallel", "arbitrary")),
        name="attn_prompt",
    )(slopes, lq, qT, kb, vT, sub_g_col, gate)


def _attn_sample_kernel(slopes_ref, lq_ref, qT_ref, ck_ref, kn_ref, cv_ref, vn_ref, g_ref, gate_ref,
                        o_ref, *, past, lam_init):
    h = pl.program_id(1)
    slope = slopes_ref[h] * LOG2E
    ts = kn_ref.shape[1]
    k_all = jnp.concatenate([ck_ref[0].astype(BF16), kn_ref[0]], axis=0)
    v_all = jnp.concatenate([cv_ref[0].astype(BF16), vn_ref[0].astype(BF16)], axis=0)
    qt = qT_ref[0, 0]
    row = lax.broadcasted_iota(jnp.int32, qt.shape, 0)
    zero = jnp.zeros_like(qt)
    kpos = lax.broadcasted_iota(jnp.int32, (past + ts, ts), 0)
    qpos = past + lax.broadcasted_iota(jnp.int32, (past + ts, ts), 1)
    bias = -slope * jnp.abs(qpos - kpos).astype(F32)
    visible = (kpos // CHUNK) <= (qpos // CHUNK)

    def softmax_t(q_masked):
        s = jnp.dot(k_all, q_masked, preferred_element_type=F32)
        t = jnp.where(visible, s + bias, NEG_BIG)
        p = jnp.exp2(t - jnp.max(t, axis=0, keepdims=True))
        return p / jnp.sum(p, axis=0, keepdims=True)

    p1 = softmax_t(jnp.where(row < ATTN_HEAD_DIM, qt, zero))
    p2 = softmax_t(jnp.where(row >= ATTN_HEAD_DIM, qt, zero))
    lam = _lambda_full(lq_ref, lam_init)
    a = (p1 - lam * p2).astype(BF16)
    o = lax.dot_general(a, v_all, (((0,), (0,)), ((), ())), preferred_element_type=F32)
    ms = jnp.mean(o * o, axis=-1, keepdims=True)
    on = o * lax.rsqrt(ms + EPS) * g_ref[...] * (1.0 - lam_init)
    o_ref[0] = on * gate_ref[0]


def _attention_sample(slopes, lq, qT, cache_k, kb, cache_v, v32, sub_g_row, gate, lam_init):
    b, heads, _, ts = qT.shape
    past = cache_k.shape[1]
    kern = functools.partial(_attn_sample_kernel, past=past, lam_init=lam_init)
    col = lambda bi, hi: (bi, 0, hi)
    return pl.pallas_call(
        kern,
        grid=(b, heads),
        in_specs=[
            pl.BlockSpec(memory_space=pltpu.SMEM),
            pl.BlockSpec((4, ATTN_HEAD_DIM), lambda bi, hi: (0, 0)),
            pl.BlockSpec((1, 1, 128, ts), lambda bi, hi: (bi, hi, 0, 0)),
            pl.BlockSpec((1, past, 128), col),
            pl.BlockSpec((1, ts, 128), col),
            pl.BlockSpec((1, past, 128), col),
            pl.BlockSpec((1, ts, 128), col),
            pl.BlockSpec((1, 128), lambda bi, hi: (0, 0)),
            pl.BlockSpec((1, ts, 128), col),
        ],
        out_specs=pl.BlockSpec((1, ts, 128), col),
        out_shape=jax.ShapeDtypeStruct((b, ts, heads * 128), F32),
        compiler_params=_cparams(("parallel", "parallel")),
        name="attn_sample",
    )(slopes, lq, qT, cache_k, kb, cache_v, v32, sub_g_row, gate)


def _rec_prep_kernel(rq_ref, rk_ref, rlf_ref, rv_ref,
                     oi_ref, qe_ref, kd_ref, el_ref):
    C = CHUNK
    n_chunks = rq_ref.shape[1] // C
    t_idx = lax.broadcasted_iota(jnp.int32, (C, C), 0)
    s_idx = lax.broadcasted_iota(jnp.int32, (C, C), 1)
    masks = [t_idx == s_idx]
    for n in LEVEL_HALVES:
        same = (t_idx // (2 * n)) == (s_idx // (2 * n))
        masks.append(same & ((t_idx % (2 * n)) >= n) & ((s_idx % (2 * n)) < n))
    tri = (s_idx <= t_idx).astype(F32).astype(BF16)
    w = rq_ref.shape[2]
    sub8 = lax.broadcasted_iota(jnp.int32, (C // 8, 8, w), 1)
    row_odd = (lax.broadcasted_iota(jnp.int32, (C, w), 0) & 1) == 1

    def midpoint(cum, n):
        if n >= 4:
            g = cum.reshape(C // (2 * n), 2 * n, w)
            return jnp.broadcast_to(g[:, n - 1:n, :], g.shape).reshape(C, w)
        g = cum.reshape(C // 8, 8, w)
        lo = jnp.broadcast_to(g[:, 1:2, :], g.shape)
        hi = jnp.broadcast_to(g[:, 5:6, :], g.shape)
        return jnp.where(sub8 < 4, lo, hi).reshape(C, w)

    def chunk(c, carry):
        rows = pl.ds(pl.multiple_of(c * C, C), C)
        q = rq_ref[0, rows, :]
        k = rk_ref[0, rows, :]
        v = rv_ref[0, rows, :].astype(BF16)
        lf = rlf_ref[0, rows, :]
        lf_hi = lf.astype(BF16)
        rem = lf - lf_hi.astype(F32)
        lf_mid = rem.astype(BF16)
        lf_lo = (rem - lf_mid.astype(F32)).astype(BF16)
        cum = (jnp.dot(tri, lf_hi, preferred_element_type=F32)
               + jnp.dot(tri, lf_mid, preferred_element_type=F32)
               + jnp.dot(tri, lf_lo, preferred_element_type=F32))
        last = cum[C - 1:C, :]
        qe_ref[0, rows, :] = (q * jnp.exp(cum)).astype(BF16)
        kd_ref[0, rows, :] = (k * jnp.exp(last - cum)).astype(BF16)
        el_ref[0, pl.ds(c, 1), :] = jnp.exp(last)
        q_lv = [q.astype(BF16)]
        k_lv = [k.astype(BF16)]
        for n in LEVEL_HALVES:
            if n == 1:
                e = jnp.where(row_odd, jnp.exp(lf), 1.0)
            else:
                e = jnp.exp(-jnp.abs(cum - midpoint(cum, n)))
            q_lv.append((q * e).astype(BF16))
            k_lv.append((k * e).astype(BF16))
        for h in range(REC_HEADS):
            cols = slice(h * REC_DIM, (h + 1) * REC_DIM)
            a = jnp.zeros((C, C), F32)
            for ql, kl, mk in zip(q_lv, k_lv, masks):
                x = lax.dot_general(ql[:, cols], kl[:, cols], (((1,), (1,)), ((), ())),
                                    preferred_element_type=F32)
                a = jnp.where(mk, x, a)
            oi_ref[0, rows, cols] = jnp.dot(a.astype(BF16), v[:, cols], preferred_element_type=F32)
        return carry

    lax.fori_loop(0, n_chunks, chunk, 0, unroll=2 if n_chunks % 2 == 0 else 1)


def _rec_prep(rq, rk, rlf, rv, tr):
    b, s, w = rq.shape
    nt = s // tr
    ncs = tr // CHUNK
    row = lambda bi, ti: (bi, ti, 0)
    rows_spec = pl.BlockSpec((1, tr, w), row)
    return pl.pallas_call(
        _rec_prep_kernel,
        grid=(b, nt),
        in_specs=[rows_spec] * 4,
        out_specs=[rows_spec, rows_spec, rows_spec, pl.BlockSpec((1, ncs, w), row)],
        out_shape=[jax.ShapeDtypeStruct((b, s, w), F32),
                   jax.ShapeDtypeStruct((b, s, w), BF16),
                   jax.ShapeDtypeStruct((b, s, w), BF16),
                   jax.ShapeDtypeStruct((b, s // CHUNK, w), F32)],
        compiler_params=_cparams(("parallel", "parallel")),
        name="rec_prep",
    )(rq, rk, rlf, rv)


def _rec_seq_kernel(oi_ref, qe_ref, kd_ref, rv_ref, el_ref, gate_ref, h0_ref, g_ref,
                    o_ref, h_ref, st_s):
    C = CHUNK
    ti = pl.program_id(1)
    n_chunks = oi_ref.shape[1] // C

    @pl.when(ti == 0)
    def _():
        for h in range(REC_HEADS):
            st_s[h] = h0_ref[0, h].T

    for c in range(n_chunks):
        rows = slice(c * C, (c + 1) * C)
        for h in range(REC_HEADS):
            cols = slice(h * REC_DIM, (h + 1) * REC_DIM)
            st = st_s[h]
            o = oi_ref[0, rows, cols] + lax.dot_general(
                qe_ref[0, rows, cols], st.astype(BF16), (((1,), (1,)), ((), ())),
                preferred_element_type=F32)
            upd = lax.dot_general(rv_ref[0, rows, cols].astype(BF16), kd_ref[0, rows, cols],
                                  (((0,), (0,)), ((), ())), preferred_element_type=F32)
            st_s[h] = st * el_ref[0, c:c + 1, cols] + upd
            ms = jnp.mean(o * o, axis=-1, keepdims=True)
            on = o * lax.rsqrt(ms + EPS) * g_ref[...]
            o_ref[0, rows, cols] = on * gate_ref[0, rows, cols]

    @pl.when(ti == pl.num_programs(1) - 1)
    def _():
        for h in range(REC_HEADS):
            h_ref[0, h] = st_s[h].T


def _rec_seq(oi, qe, kd, rv, el, gate, h0, rec_g_row, tr):
    b, s, w = oi.shape
    nt = s // tr
    ncs = tr // CHUNK
    row = lambda bi, ti: (bi, ti, 0)
    rows_spec = pl.BlockSpec((1, tr, w), row)
    st_spec = pl.BlockSpec((1, REC_HEADS, REC_DIM, REC_DIM), lambda bi, ti: (bi, 0, 0, 0))
    return pl.pallas_call(
        _rec_seq_kernel,
        grid=(b, nt),
        in_specs=[rows_spec, rows_spec, rows_spec, rows_spec,
                  pl.BlockSpec((1, ncs, w), row), rows_spec, st_spec,
                  pl.BlockSpec((1, REC_DIM), lambda bi, ti: (0, 0))],
        out_specs=[rows_spec, st_spec],
        out_shape=[jax.ShapeDtypeStruct((b, s, w), F32),
                   jax.ShapeDtypeStruct((b, REC_HEADS, REC_DIM, REC_DIM), F32)],
        scratch_shapes=[pltpu.VMEM((REC_HEADS, REC_DIM, REC_DIM), F32)],
        compiler_params=_cparams(("parallel", "arbitrary")),
        name="rec_seq",
    )(oi, qe, kd, rv, el, gate, h0, rec_g_row)


def _merge_kernel(x_ref, ma_ref, mr_ref, w_ref, g_ref, y_ref):
    half = ma_ref.shape[2]
    y = x_ref[0]
    y = y + jnp.dot(ma_ref[0].astype(BF16), w_ref[:half, :], preferred_element_type=F32)
    y = y + jnp.dot(mr_ref[0].astype(BF16), w_ref[half:, :], preferred_element_type=F32)
    ms = jnp.mean(y * y, axis=-1, keepdims=True)
    y_ref[0] = y * lax.rsqrt(ms + EPS) * g_ref[...]


def _merge(x, mix_a, mix_r, w_out_bf16, final_g, tm):
    b, s, d = x.shape
    nt = s // tm
    row = lambda bi, ti: (bi, ti, 0)
    return pl.pallas_call(
        _merge_kernel,
        grid=(b, nt),
        in_specs=[
            pl.BlockSpec((1, tm, d), row),
            pl.BlockSpec((1, tm, mix_a.shape[2]), row),
            pl.BlockSpec((1, tm, mix_r.shape[2]), row),
            pl.BlockSpec(w_out_bf16.shape, lambda bi, ti: (0, 0)),
            pl.BlockSpec((1, d), lambda bi, ti: (0, 0)),
        ],
        out_specs=pl.BlockSpec((1, tm, d), row),
        out_shape=jax.ShapeDtypeStruct((b, s, d), F32),
        compiler_params=_cparams(("parallel", "parallel")),
        name="merge",
    )(x, mix_a, mix_r, w_out_bf16, final_g)


def kernel(x_prompt, x_sample, cache_k, cache_v, state_h, norm_g, w_in, lambda_qk, subln_g, rec_lb,
           rec_norm_g, w_out, final_g):
    depth = w_in.shape[0]
    assert depth == 1
    l = 0
    lam_init = 0.8 - 0.6 * math.exp(-0.3 * l)
    slopes = jnp.exp2(-8.0 * jnp.arange(1, ATTN_HEADS + 1, dtype=F32) / ATTN_HEADS)
    bp, sp, _ = x_prompt.shape
    bs, ts, _ = x_sample.shape
    past = cache_k.shape[2]

    w_in_b = w_in[l].astype(BF16)
    w_out_b = w_out[l].astype(BF16)
    g_in = norm_g[l].reshape(1, D_MODEL)
    lq = lambda_qk[l]
    sub_g = subln_g[l]
    rec_g = rec_norm_g[l].reshape(1, REC_DIM)
    fin_g = final_g.reshape(1, D_MODEL)
    lb_rows = rec_lb[l:l + 2]

    tm = _row_tile(sp, 256)
    (qT, kb, vT, k32, v32, ga, rq, rk, rlf, rv, gr) = _project(x_prompt, g_in, w_in_b, lb_rows, tm)
    mix_a = _attention_prompt(slopes, lq, qT, kb, vT, sub_g.reshape(128, 1), ga, lam_init)
    tr = _row_tile(sp, REC_TILE)
    oi, qe, kd, el = _rec_prep(rq, rk, rlf, rv, tr)
    h0 = jnp.zeros((bp, REC_HEADS, REC_DIM, REC_DIM), F32)
    mix_r, h_p = _rec_seq(oi, qe, kd, rv, el, gr, h0, rec_g, tr)
    y_prompt = _merge(x_prompt, mix_a, mix_r, w_out_b, fin_g, _row_tile(sp, 512))

    (qT_s, kb_s, _, k32_s, v32_s, ga_s, rq_s, rk_s, rlf_s, rv_s, gr_s) = _project(
        x_sample, g_in, w_in_b, lb_rows, ts)
    ck = cache_k[l].reshape(bs, past, ATTN_HEADS * 128)
    cv = cache_v[l].reshape(bs, past, ATTN_HEADS * 128)
    mix_a_s = _attention_sample(slopes, lq, qT_s, ck, kb_s, cv, v32_s, sub_g.reshape(1, 128), ga_s, lam_init)
    oi_s, qe_s, kd_s, el_s = _rec_prep(rq_s, rk_s, rlf_s, rv_s, ts)
    mix_r_s, h_s = _rec_seq(oi_s, qe_s, kd_s, rv_s, el_s, gr_s, state_h[l].astype(F32), rec_g, ts)
    y_sample = _merge(x_sample, mix_a_s, mix_r_s, w_out_b, fin_g, ts)

    shape_kv = lambda a, b_, s_: a.reshape(1, b_, s_, ATTN_HEADS, ATTN_VAL_DIM)
    return (y_prompt, y_sample,
            shape_kv(k32, bp, sp), shape_kv(v32, bp, sp), h_p[None],
            shape_kv(k32_s, bs, ts), shape_kv(v32_s, bs, ts), h_s[None])
```

```python
import functools
import math

import jax
import jax.numpy as jnp
from jax import lax
from jax.experimental import pallas as pl
from jax.experimental.pallas import tpu as pltpu

F32 = jnp.float32
BF16 = jnp.bfloat16

D_MODEL = 1024
CHUNK = 64
ATTN_HEADS = 4
ATTN_HEAD_DIM = 64
ATTN_VAL_DIM = 2 * ATTN_HEAD_DIM
REC_HEADS = 4
REC_DIM = 128
GROUP_WIDTH = 512
N_GROUPS = 8
EPS = 1e-6
LOG2E = math.log2(math.e)
ATTN_SCALE = ATTN_HEAD_DIM ** -0.5
NEG_BIG = -1e30
M_INIT = -1e20

VMEM_LIMIT_BYTES = 52 * 1024 * 1024
ATTN_TILE = 512
ATTN_COLS = 256
ATTN_ROWS = 128
L_ROWS = 16
REC_TILE = 512
LEVEL_HALVES = (32, 16, 8, 4, 2, 1)


def _row_tile(seq, want):
    return want if seq % want == 0 else seq


def _cparams(sem):
    return pltpu.CompilerParams(dimension_semantics=sem, vmem_limit_bytes=VMEM_LIMIT_BYTES)


def _sigmoid(x):
    return 1.0 / (1.0 + jnp.exp(-x))


def _proj_kernel(x_ref, g_ref, w_ref, lb_ref,
                 qT_ref, kb_ref, vT_ref, k32_ref, v32_ref, ga_ref,
                 rq_ref, rk_ref, rlf_ref, rv_ref, gr_ref):
    x = x_ref[0]
    ms = jnp.mean(x * x, axis=-1, keepdims=True)
    xn = (x * lax.rsqrt(ms + EPS) * g_ref[...]).astype(BF16)

    def proj(c):
        return jnp.dot(xn, w_ref[:, c * GROUP_WIDTH:(c + 1) * GROUP_WIDTH],
                       preferred_element_type=F32)

    qa = proj(0) * (ATTN_SCALE * LOG2E)
    for h in range(ATTN_HEADS):
        qT_ref[0, h] = qa[:, h * 128:(h + 1) * 128].T.astype(BF16)
    ka = proj(1)
    k32_ref[0] = ka
    kb_ref[0] = ka.astype(BF16)
    va = proj(2)
    v32_ref[0] = va
    for h in range(ATTN_HEADS):
        vT_ref[0, h] = va[:, h * 128:(h + 1) * 128].T.astype(BF16)
    za = proj(3)
    ga_ref[0] = za * _sigmoid(za)
    qr = proj(4)
    rq_ref[0] = qr * _sigmoid(qr)
    r0 = lb_ref[0:1, :]
    r1 = lb_ref[1:2, :]
    rmax = jnp.maximum(r0, r1)
    e0 = jnp.exp(r0 - rmax)
    e1 = jnp.exp(r1 - rmax)
    lb = e0 / (e0 + e1)
    fr = proj(5)
    key = (1.0 - lb) * _sigmoid(-fr)
    rk_ref[0] = key
    rlf_ref[0] = jnp.log1p(-key)
    rv_ref[0] = proj(6)
    zr = proj(7)
    gr_ref[0] = zr * _sigmoid(zr)


def _project(x, norm_g, w_bf16, rec_lb, tm):
    b, s, d = x.shape
    nt = s // tm
    row = lambda bi, ti: (bi, ti, 0)
    rowT = lambda bi, ti: (bi, 0, 0, ti)
    f32_rows = jax.ShapeDtypeStruct((b, s, GROUP_WIDTH), F32)
    bf_rows = jax.ShapeDtypeStruct((b, s, GROUP_WIDTH), BF16)
    bf_T = jax.ShapeDtypeStruct((b, ATTN_HEADS, 128, s), BF16)
    rows_spec = pl.BlockSpec((1, tm, GROUP_WIDTH), row)
    T_spec = pl.BlockSpec((1, ATTN_HEADS, 128, tm), rowT)
    return pl.pallas_call(
        _proj_kernel,
        grid=(b, nt),
        in_specs=[
            pl.BlockSpec((1, tm, d), row),
            pl.BlockSpec((1, d), lambda bi, ti: (0, 0)),
            pl.BlockSpec((d, N_GROUPS * GROUP_WIDTH), lambda bi, ti: (0, 0)),
            pl.BlockSpec((2, GROUP_WIDTH), lambda bi, ti: (0, 0)),
        ],
        out_specs=[T_spec, rows_spec, T_spec, rows_spec, rows_spec, rows_spec,
                   rows_spec, rows_spec, rows_spec, rows_spec, rows_spec],
        out_shape=[bf_T, bf_rows, bf_T, f32_rows, f32_rows, f32_rows,
                   f32_rows, f32_rows, f32_rows, f32_rows, f32_rows],
        compiler_params=_cparams(("parallel", "parallel")),
        name="proj",
    )(x, norm_g, w_bf16, rec_lb)


def _lambda_full(lq_ref, lam_init):
    lq = lq_ref[...]
    s01 = jnp.sum(lq[0:1] * lq[1:2], axis=-1, keepdims=True)
    s23 = jnp.sum(lq[2:3] * lq[3:4], axis=-1, keepdims=True)
    return jnp.exp(s01) - jnp.exp(s23) + lam_init


def _attn_kernel(slopes_ref, lq_ref, qT_ref, k_ref, vT_ref, g_ref, gate_ref,
                 o_ref,
                 qt_s, bias_s, s_a, s_b, p_a, p_b, mb_a, mb_b, al_a, al_b, m_s, acc_s, *, tile, lam_init):
    T = tile
    W = 2 * T
    CW = ATTN_COLS
    RC = ATTN_ROWS
    col_chunks = [slice(c * CW, (c + 1) * CW) for c in range(W // CW)]
    row_chunks = [slice(r * RC, (r + 1) * RC) for r in range(T // RC)]
    h = pl.program_id(1)
    qi = pl.program_id(2)
    slope = slopes_ref[h] * LOG2E
    shift = slope * T

    @pl.when(qi == 0)
    def _():
        for c, cols in enumerate(col_chunks):
            jl = lax.broadcasted_iota(jnp.int32, (T, CW), 0)
            il = (lax.broadcasted_iota(jnp.int32, (T, CW), 1) + c * CW) & (T - 1)
            before = slope * (jl - T).astype(F32)
            after = slope * (2 * il - jl - T).astype(F32)
            visible = (jl // CHUNK) <= (il // CHUNK)
            bias_s[0, :, cols] = before
            bias_s[1, :, cols] = jnp.where(visible, jnp.where(jl <= il, before, after), NEG_BIG)
            bias_s[2, :, cols] = jnp.full((T, CW), NEG_BIG, F32)

    qt = qT_ref[0, 0]
    row = lax.broadcasted_iota(jnp.int32, qt.shape, 0)
    zero = jnp.zeros_like(qt)
    qt_s[:, :T] = jnp.where(row < ATTN_HEAD_DIM, qt, zero)
    qt_s[:, T:] = jnp.where(row >= ATTN_HEAD_DIM, qt, zero)
    m_s[...] = jnp.full(m_s.shape, M_INIT, F32)
    acc_s[...] = jnp.zeros(acc_s.shape, F32)
    p_b[...] = jnp.zeros(p_b.shape, BF16)
    al_b[...] = jnp.ones(al_b.shape, F32)

    def stage_s(i, s_ref, mb_ref):
        blk = jnp.minimum(i, qi)
        bidx = jnp.clip(i - qi + 1, 0, 2)
        start = blk * T
        for cols in col_chunks:
            w = qt_s[:, cols]
            mx = None
            for rows in row_chunks:
                kt = k_ref[0, pl.ds(pl.multiple_of(start + rows.start, RC), RC), :]
                s = jnp.dot(kt, w, preferred_element_type=F32) + bias_s[bidx, rows, cols]
                s_ref[rows, cols] = s
                part = jnp.max(s.reshape(RC // 8, 8, CW), axis=0)
                mx = part if mx is None else jnp.maximum(mx, part)
            mb_ref[:, cols] = mx

    def stage_x(s_ref, mb_ref, p_ref, al_ref):
        for cols in col_chunks:
            m_prev = m_s[:, cols] - shift
            m_new = jnp.maximum(m_prev, jnp.max(mb_ref[:, cols], axis=0, keepdims=True))
            for rows in row_chunks:
                p_ref[rows, cols] = jnp.exp2(s_ref[rows, cols] - m_new).astype(BF16)
            al_ref[:, cols] = jnp.exp2(m_prev - m_new)
            m_s[:, cols] = m_new

    ones_rows = jnp.ones((L_ROWS, T), BF16)

    def stage_p(i, p_ref, al_ref):
        blk = jnp.clip(i, 0, qi)
        vt = vT_ref[0, 0, :, pl.ds(pl.multiple_of(blk * T, T), T)]
        vt1 = jnp.concatenate([vt, ones_rows], axis=0)
        for cols in col_chunks:
            acc_s[:, cols] = al_ref[:, cols] * acc_s[:, cols] + jnp.dot(
                vt1, p_ref[:, cols], preferred_element_type=F32)

    n_pairs = (qi + 2) // 2

    stage_s(0, s_a, mb_a)

    def body(u, carry):
        i = 2 * u
        stage_x(s_a, mb_a, p_a, al_a)
        stage_s(i + 1, s_b, mb_b)
        stage_p(i - 1, p_b, al_b)
        stage_x(s_b, mb_b, p_b, al_b)
        stage_s(i + 2, s_a, mb_a)
        stage_p(i, p_a, al_a)
        return carry

    lax.fori_loop(0, n_pairs, body, 0)
    stage_p(2 * n_pairs - 1, p_b, al_b)

    lam = _lambda_full(lq_ref, lam_init)
    accn = acc_s[0:ATTN_VAL_DIM, :] / acc_s[ATTN_VAL_DIM:ATTN_VAL_DIM + 1, :]
    oT = accn[:, :T] - lam * accn[:, T:]
    ms = jnp.mean(oT * oT, axis=0, keepdims=True)
    on = oT * lax.rsqrt(ms + EPS) * g_ref[...] * (1.0 - lam_init)
    o_ref[0] = on.T * gate_ref[0]


def _attention_prompt(slopes, lq, qT, kb, vT, sub_g_col, gate, lam_init):
    b, heads, _, s = qT.shape
    T = _row_tile(s, ATTN_TILE)
    assert T % ATTN_COLS == 0 and T % CHUNK == 0 and (T & (T - 1)) == 0
    nq = s // T
    kern = functools.partial(_attn_kernel, tile=T, lam_init=lam_init)
    return pl.pallas_call(
        kern,
        grid=(b, heads, nq),
        in_specs=[
            pl.BlockSpec(memory_space=pltpu.SMEM),
            pl.BlockSpec((4, ATTN_HEAD_DIM), lambda bi, hi, qi: (0, 0)),
            pl.BlockSpec((1, 1, 128, T), lambda bi, hi, qi: (bi, hi, 0, qi)),
            pl.BlockSpec((1, s, 128), lambda bi, hi, qi: (bi, 0, hi)),
            pl.BlockSpec((1, 1, 128, s), lambda bi, hi, qi: (bi, hi, 0, 0)),
            pl.BlockSpec((128, 1), lambda bi, hi, qi: (0, 0)),
            pl.BlockSpec((1, T, 128), lambda bi, hi, qi: (bi, qi, hi)),
        ],
        out_specs=pl.BlockSpec((1, T, 128), lambda bi, hi, qi: (bi, qi, hi)),
        out_shape=jax.ShapeDtypeStruct((b, s, heads * 128), F32),
        scratch_shapes=[
            pltpu.VMEM((128, 2 * T), BF16),
            pltpu.VMEM((3, T, 2 * T), F32),
            pltpu.VMEM((T, 2 * T), F32),
            pltpu.VMEM((T, 2 * T), F32),
            pltpu.VMEM((T, 2 * T), BF16),
            pltpu.VMEM((T, 2 * T), BF16),
            pltpu.VMEM((8, 2 * T), F32),
            pltpu.VMEM((8, 2 * T), F32),
            pltpu.VMEM((1, 2 * T), F32),
            pltpu.VMEM((1, 2 * T), F32),
            pltpu.VMEM((1, 2 * T), F32),
            pltpu.VMEM((ATTN_VAL_DIM + L_ROWS, 2 * T), F32),
        ],
        compiler_params=_cparams(("parallel", "parallel", "arbitrary")),
        name="attn_prompt",
    )(slopes, lq, qT, kb, vT, sub_g_col, gate)


def _attn_sample_kernel(slopes_ref, lq_ref, qT_ref, ck_ref, kn_ref, cv_ref, vn_ref, g_ref, gate_ref,
                        o_ref, *, past, lam_init):
    h = pl.program_id(1)
    slope = slopes_ref[h] * LOG2E
    ts = kn_ref.shape[1]
    k_all = jnp.concatenate([ck_ref[0].astype(BF16), kn_ref[0]], axis=0)
    v_all = jnp.concatenate([cv_ref[0].astype(BF16), vn_ref[0].astype(BF16)], axis=0)
    qt = qT_ref[0, 0]
    row = lax.broadcasted_iota(jnp.int32, qt.shape, 0)
    zero = jnp.zeros_like(qt)
    kpos = lax.broadcasted_iota(jnp.int32, (past + ts, ts), 0)
    qpos = past + lax.broadcasted_iota(jnp.int32, (past + ts, ts), 1)
    bias = -slope * jnp.abs(qpos - kpos).astype(F32)
    visible = (kpos // CHUNK) <= (qpos // CHUNK)

    def softmax_t(q_masked):
        s = jnp.dot(k_all, q_masked, preferred_element_type=F32)
        t = jnp.where(visible, s + bias, NEG_BIG)
        p = jnp.exp2(t - jnp.max(t, axis=0, keepdims=True))
        return p / jnp.sum(p, axis=0, keepdims=True)

    p1 = softmax_t(jnp.where(row < ATTN_HEAD_DIM, qt, zero))
    p2 = softmax_t(jnp.where(row >= ATTN_HEAD_DIM, qt, zero))
    lam = _lambda_full(lq_ref, lam_init)
    a = (p1 - lam * p2).astype(BF16)
    o = lax.dot_general(a, v_all, (((0,), (0,)), ((), ())), preferred_element_type=F32)
    ms = jnp.mean(o * o, axis=-1, keepdims=True)
    on = o * lax.rsqrt(ms + EPS) * g_ref[...] * (1.0 - lam_init)
    o_ref[0] = on * gate_ref[0]


def _attention_sample(slopes, lq, qT, cache_k, kb, cache_v, v32, sub_g_row, gate, lam_init):
    b, heads, _, ts = qT.shape
    past = cache_k.shape[1]
    kern = functools.partial(_attn_sample_kernel, past=past, lam_init=lam_init)
    col = lambda bi, hi: (bi, 0, hi)
    return pl.pallas_call(
        kern,
        grid=(b, heads),
        in_specs=[
            pl.BlockSpec(memory_space=pltpu.SMEM),
            pl.BlockSpec((4, ATTN_HEAD_DIM), lambda bi, hi: (0, 0)),
            pl.BlockSpec((1, 1, 128, ts), lambda bi, hi: (bi, hi, 0, 0)),
            pl.BlockSpec((1, past, 128), col),
            pl.BlockSpec((1, ts, 128), col),
            pl.BlockSpec((1, past, 128), col),
            pl.BlockSpec((1, ts, 128), col),
            pl.BlockSpec((1, 128), lambda bi, hi: (0, 0)),
            pl.BlockSpec((1, ts, 128), col),
        ],
        out_specs=pl.BlockSpec((1, ts, 128), col),
        out_shape=jax.ShapeDtypeStruct((b, ts, heads * 128), F32),
        compiler_params=_cparams(("parallel", "parallel")),
        name="attn_sample",
    )(slopes, lq, qT, cache_k, kb, cache_v, v32, sub_g_row, gate)


def _rec_prep_kernel(rq_ref, rk_ref, rlf_ref, rv_ref,
                     oi_ref, qe_ref, kd_ref, el_ref):
    C = CHUNK
    n_chunks = rq_ref.shape[1] // C
    t_idx = lax.broadcasted_iota(jnp.int32, (C, C), 0)
    s_idx = lax.broadcasted_iota(jnp.int32, (C, C), 1)
    masks = [t_idx == s_idx]
    for n in LEVEL_HALVES:
        same = (t_idx // (2 * n)) == (s_idx // (2 * n))
        masks.append(same & ((t_idx % (2 * n)) >= n) & ((s_idx % (2 * n)) < n))
    tri = (s_idx <= t_idx).astype(F32).astype(BF16)
    w = rq_ref.shape[2]
    sub8 = lax.broadcasted_iota(jnp.int32, (C // 8, 8, w), 1)
    row_odd = (lax.broadcasted_iota(jnp.int32, (C, w), 0) & 1) == 1

    def midpoint(cum, n):
        if n >= 4:
            g = cum.reshape(C // (2 * n), 2 * n, w)
            return jnp.broadcast_to(g[:, n - 1:n, :], g.shape).reshape(C, w)
        g = cum.reshape(C // 8, 8, w)
        lo = jnp.broadcast_to(g[:, 1:2, :], g.shape)
        hi = jnp.broadcast_to(g[:, 5:6, :], g.shape)
        return jnp.where(sub8 < 4, lo, hi).reshape(C, w)

    def chunk(c, carry):
        rows = pl.ds(pl.multiple_of(c * C, C), C)
        q = rq_ref[0, rows, :]
        k = rk_ref[0, rows, :]
        v = rv_ref[0, rows, :].astype(BF16)
        lf = rlf_ref[0, rows, :]
        lf_hi = lf.astype(BF16)
        rem = lf - lf_hi.astype(F32)
        lf_mid = rem.astype(BF16)
        lf_lo = (rem - lf_mid.astype(F32)).astype(BF16)
        cum = (jnp.dot(tri, lf_hi, preferred_element_type=F32)
               + jnp.dot(tri, lf_mid, preferred_element_type=F32)
               + jnp.dot(tri, lf_lo, preferred_element_type=F32))
        last = cum[C - 1:C, :]
        qe_ref[0, rows, :] = (q * jnp.exp(cum)).astype(BF16)
        kd_ref[0, rows, :] = (k * jnp.exp(last - cum)).astype(BF16)
        el_ref[0, pl.ds(c, 1), :] = jnp.exp(last)
        q_lv = [q.astype(BF16)]
        k_lv = [k.astype(BF16)]
        for n in LEVEL_HALVES:
            if n == 1:
                e = jnp.where(row_odd, jnp.exp(lf), 1.0)
            else:
                e = jnp.exp(-jnp.abs(cum - midpoint(cum, n)))
            q_lv.append((q * e).astype(BF16))
            k_lv.append((k * e).astype(BF16))
        for h in range(REC_HEADS):
            cols = slice(h * REC_DIM, (h + 1) * REC_DIM)
            a = jnp.zeros((C, C), F32)
            for ql, kl, mk in zip(q_lv, k_lv, masks):
                x = lax.dot_general(ql[:, cols], kl[:, cols], (((1,), (1,)), ((), ())),
                                    preferred_element_type=F32)
                a = jnp.where(mk, x, a)
            oi_ref[0, rows, cols] = jnp.dot(a.astype(BF16), v[:, cols], preferred_element_type=F32)
        return carry

    lax.fori_loop(0, n_chunks, chunk, 0, unroll=2 if n_chunks % 2 == 0 else 1)


def _rec_prep(rq, rk, rlf, rv, tr):
    b, s, w = rq.shape
    nt = s // tr
    ncs = tr // CHUNK
    row = lambda bi, ti: (bi, ti, 0)
    rows_spec = pl.BlockSpec((1, tr, w), row)
    return pl.pallas_call(
        _rec_prep_kernel,
        grid=(b, nt),
        in_specs=[rows_spec] * 4,
        out_specs=[rows_spec, rows_spec, rows_spec, pl.BlockSpec((1, ncs, w), row)],
        out_shape=[jax.ShapeDtypeStruct((b, s, w), F32),
                   jax.ShapeDtypeStruct((b, s, w), BF16),
                   jax.ShapeDtypeStruct((b, s, w), BF16),
                   jax.ShapeDtypeStruct((b, s // CHUNK, w), F32)],
        compiler_params=_cparams(("parallel", "parallel")),
        name="rec_prep",
    )(rq, rk, rlf, rv)


def _rec_seq_kernel(oi_ref, qe_ref, kd_ref, rv_ref, el_ref, gate_ref, h0_ref, g_ref,
                    o_ref, h_ref, st_s):
    C = CHUNK
    ti = pl.program_id(1)
    n_chunks = oi_ref.shape[1] // C

    @pl.when(ti == 0)
    def _():
        for h in range(REC_HEADS):
            st_s[h] = h0_ref[0, h].T

    for c in range(n_chunks):
        rows = slice(c * C, (c + 1) * C)
        for h in range(REC_HEADS):
            cols = slice(h * REC_DIM, (h + 1) * REC_DIM)
            st = st_s[h]
            o = oi_ref[0, rows, cols] + lax.dot_general(
                qe_ref[0, rows, cols], st.astype(BF16), (((1,), (1,)), ((), ())),
                preferred_element_type=F32)
            upd = lax.dot_general(rv_ref[0, rows, cols].astype(BF16), kd_ref[0, rows, cols],
                                  (((0,), (0,)), ((), ())), preferred_element_type=F32)
            st_s[h] = st * el_ref[0, c:c + 1, cols] + upd
            ms = jnp.mean(o * o, axis=-1, keepdims=True)
            on = o * lax.rsqrt(ms + EPS) * g_ref[...]
            o_ref[0, rows, cols] = on * gate_ref[0, rows, cols]

    @pl.when(ti == pl.num_programs(1) - 1)
    def _():
        for h in range(REC_HEADS):
            h_ref[0, h] = st_s[h].T


def _rec_seq(oi, qe, kd, rv, el, gate, h0, rec_g_row, tr):
    b, s, w = oi.shape
    nt = s // tr
    ncs = tr // CHUNK
    row = lambda bi, ti: (bi, ti, 0)
    rows_spec = pl.BlockSpec((1, tr, w), row)
    st_spec = pl.BlockSpec((1, REC_HEADS, REC_DIM, REC_DIM), lambda bi, ti: (bi, 0, 0, 0))
    return pl.pallas_call(
        _rec_seq_kernel,
        grid=(b, nt),
        in_specs=[rows_spec, rows_spec, rows_spec, rows_spec,
                  pl.BlockSpec((1, ncs, w), row), rows_spec, st_spec,
                  pl.BlockSpec((1, REC_DIM), lambda bi, ti: (0, 0))],
        out_specs=[rows_spec, st_spec],
        out_shape=[jax.ShapeDtypeStruct((b, s, w), F32),
                   jax.ShapeDtypeStruct((b, REC_HEADS, REC_DIM, REC_DIM), F32)],
        scratch_shapes=[pltpu.VMEM((REC_HEADS, REC_DIM, REC_DIM), F32)],
        compiler_params=_cparams(("parallel", "arbitrary")),
        name="rec_seq",
    )(oi, qe, kd, rv, el, gate, h0, rec_g_row)


def _merge_kernel(x_ref, ma_ref, mr_ref, w_ref, g_ref, y_ref):
    half = ma_ref.shape[2]
    y = x_ref[0]
    y = y + jnp.dot(ma_ref[0].astype(BF16), w_ref[:half, :], preferred_element_type=F32)
    y = y + jnp.dot(mr_ref[0].astype(BF16), w_ref[half:, :], preferred_element_type=F32)
    ms = jnp.mean(y * y, axis=-1, keepdims=True)
    y_ref[0] = y * lax.rsqrt(ms + EPS) * g_ref[...]


def _merge(x, mix_a, mix_r, w_out_bf16, final_g, tm):
    b, s, d = x.shape
    nt = s // tm
    row = lambda bi, ti: (bi, ti, 0)
    return pl.pallas_call(
        _merge_kernel,
        grid=(b, nt),
        in_specs=[
            pl.BlockSpec((1, tm, d), row),
            pl.BlockSpec((1, tm, mix_a.shape[2]), row),
            pl.BlockSpec((1, tm, mix_r.shape[2]), row),
            pl.BlockSpec(w_out_bf16.shape, lambda bi, ti: (0, 0)),
            pl.BlockSpec((1, d), lambda bi, ti: (0, 0)),
        ],
        out_specs=pl.BlockSpec((1, tm, d), row),
        out_shape=jax.ShapeDtypeStruct((b, s, d), F32),
        compiler_params=_cparams(("parallel", "parallel")),
        name="merge",
    )(x, mix_a, mix_r, w_out_bf16, final_g)


def kernel(x_prompt, x_sample, cache_k, cache_v, state_h, norm_g, w_in, lambda_qk, subln_g, rec_lb,
           rec_norm_g, w_out, final_g):
    depth = w_in.shape[0]
    assert depth == 1
    l = 0
    lam_init = 0.8 - 0.6 * math.exp(-0.3 * l)
    slopes = jnp.exp2(-8.0 * jnp.arange(1, ATTN_HEADS + 1, dtype=F32) / ATTN_HEADS)
    bp, sp, _ = x_prompt.shape
    bs, ts, _ = x_sample.shape
    past = cache_k.shape[2]

    w_in_b = w_in[l].astype(BF16)
    w_out_b = w_out[l].astype(BF16)
    g_in = norm_g[l].reshape(1, D_MODEL)
    lq = lambda_qk[l]
    sub_g = subln_g[l]
    rec_g = rec_norm_g[l].reshape(1, REC_DIM)
    fin_g = final_g.reshape(1, D_MODEL)
    lb_rows = rec_lb[l:l + 2]

    tm = _row_tile(sp, 256)
    (qT, kb, vT, k32, v32, ga, rq, rk, rlf, rv, gr) = _project(x_prompt, g_in, w_in_b, lb_rows, tm)
    mix_a = _attention_prompt(slopes, lq, qT, kb, vT, sub_g.reshape(128, 1), ga, lam_init)
    tr = _row_tile(sp, REC_TILE)
    oi, qe, kd, el = _rec_prep(rq, rk, rlf, rv, tr)
    h0 = jnp.zeros((bp, REC_HEADS, REC_DIM, REC_DIM), F32)
    mix_r, h_p = _rec_seq(oi, qe, kd, rv, el, gr, h0, rec_g, tr)
    y_prompt = _merge(x_prompt, mix_a, mix_r, w_out_b, fin_g, _row_tile(sp, 512))

    (qT_s, kb_s, _, k32_s, v32_s, ga_s, rq_s, rk_s, rlf_s, rv_s, gr_s) = _project(
        x_sample, g_in, w_in_b, lb_rows, ts)
    ck = cache_k[l].reshape(bs, past, ATTN_HEADS * 128)
    cv = cache_v[l].reshape(bs, past, ATTN_HEADS * 128)
    mix_a_s = _attention_sample(slopes, lq, qT_s, ck, kb_s, cv, v32_s, sub_g.reshape(1, 128), ga_s, lam_init)
    oi_s, qe_s, kd_s, el_s = _rec_prep(rq_s, rk_s, rlf_s, rv_s, ts)
    mix_r_s, h_s = _rec_seq(oi_s, qe_s, kd_s, rv_s, el_s, gr_s, state_h[l].astype(F32), rec_g, ts)
    y_sample = _merge(x_sample, mix_a_s, mix_r_s, w_out_b, fin_g, ts)

    shape_kv = lambda a, b_, s_: a.reshape(1, b_, s_, ATTN_HEADS, ATTN_VAL_DIM)
    return (y_prompt, y_sample,
            shape_kv(k32, bp, sp), shape_kv(v32, bp, sp), h_p[None],
            shape_kv(k32_s, bs, ts), shape_kv(v32_s, bs, ts), h_s[None])
```

```python
import functools
import math

import jax
import jax.numpy as jnp
from jax import lax
from jax.experimental import pallas as pl
from jax.experimental.pallas import tpu as pltpu

F32 = jnp.float32
BF16 = jnp.bfloat16

D_MODEL = 1024
CHUNK = 64
ATTN_HEADS = 4
ATTN_HEAD_DIM = 64
ATTN_VAL_DIM = 2 * ATTN_HEAD_DIM
REC_HEADS = 4
REC_DIM = 128
GROUP_WIDTH = 512
N_GROUPS = 8
EPS = 1e-6
LOG2E = math.log2(math.e)
ATTN_SCALE = ATTN_HEAD_DIM ** -0.5
NEG_BIG = -1e30
M_INIT = -1e20

VMEM_LIMIT_BYTES = 52 * 1024 * 1024
ATTN_TILE = 512
ATTN_COLS = 256
ATTN_ROWS = 128
L_ROWS = 16
REC_TILE = 512
LEVEL_HALVES = (32, 16, 8, 4, 2, 1)


def _row_tile(seq, want):
    return want if seq % want == 0 else seq


def _cparams(sem):
    return pltpu.CompilerParams(dimension_semantics=sem, vmem_limit_bytes=VMEM_LIMIT_BYTES)


def _sigmoid(x):
    return 1.0 / (1.0 + jnp.exp(-x))


def _proj_kernel(x_ref, g_ref, w_ref, lb_ref,
                 qT_ref, kb_ref, vT_ref, k32_ref, v32_ref, ga_ref,
                 rq_ref, rk_ref, rlf_ref, rv_ref, gr_ref):
    x = x_ref[0]
    ms = jnp.mean(x * x, axis=-1, keepdims=True)
    xn = (x * lax.rsqrt(ms + EPS) * g_ref[...]).astype(BF16)

    def proj(c):
        return jnp.dot(xn, w_ref[:, c * GROUP_WIDTH:(c + 1) * GROUP_WIDTH],
                       preferred_element_type=F32)

    qa = proj(0) * (ATTN_SCALE * LOG2E)
    for h in range(ATTN_HEADS):
        qT_ref[0, h] = qa[:, h * 128:(h + 1) * 128].T.astype(BF16)
    ka = proj(1)
    k32_ref[0] = ka
    kb_ref[0] = ka.astype(BF16)
    va = proj(2)
    v32_ref[0] = va
    for h in range(ATTN_HEADS):
        vT_ref[0, h] = va[:, h * 128:(h + 1) * 128].T.astype(BF16)
    za = proj(3)
    ga_ref[0] = za * _sigmoid(za)
    qr = proj(4)
    rq_ref[0] = qr * _sigmoid(qr)
    r0 = lb_ref[0:1, :]
    r1 = lb_ref[1:2, :]
    rmax = jnp.maximum(r0, r1)
    e0 = jnp.exp(r0 - rmax)
    e1 = jnp.exp(r1 - rmax)
    lb = e0 / (e0 + e1)
    fr = proj(5)
    key = (1.0 - lb) * _sigmoid(-fr)
    rk_ref[0] = key
    rlf_ref[0] = jnp.log1p(-key)
    rv_ref[0] = proj(6)
    zr = proj(7)
    gr_ref[0] = zr * _sigmoid(zr)


def _project(x, norm_g, w_bf16, rec_lb, tm):
    b, s, d = x.shape
    nt = s // tm
    row = lambda bi, ti: (bi, ti, 0)
    rowT = lambda bi, ti: (bi, 0, 0, ti)
    f32_rows = jax.ShapeDtypeStruct((b, s, GROUP_WIDTH), F32)
    bf_rows = jax.ShapeDtypeStruct((b, s, GROUP_WIDTH), BF16)
    bf_T = jax.ShapeDtypeStruct((b, ATTN_HEADS, 128, s), BF16)
    rows_spec = pl.BlockSpec((1, tm, GROUP_WIDTH), row)
    T_spec = pl.BlockSpec((1, ATTN_HEADS, 128, tm), rowT)
    return pl.pallas_call(
        _proj_kernel,
        grid=(b, nt),
        in_specs=[
            pl.BlockSpec((1, tm, d), row),
            pl.BlockSpec((1, d), lambda bi, ti: (0, 0)),
            pl.BlockSpec((d, N_GROUPS * GROUP_WIDTH), lambda bi, ti: (0, 0)),
            pl.BlockSpec((2, GROUP_WIDTH), lambda bi, ti: (0, 0)),
        ],
        out_specs=[T_spec, rows_spec, T_spec, rows_spec, rows_spec, rows_spec,
                   rows_spec, rows_spec, rows_spec, rows_spec, rows_spec],
        out_shape=[bf_T, bf_rows, bf_T, f32_rows, f32_rows, f32_rows,
                   f32_rows, f32_rows, f32_rows, f32_rows, f32_rows],
        compiler_params=_cparams(("parallel", "parallel")),
        name="proj",
    )(x, norm_g, w_bf16, rec_lb)


def _lambda_full(lq_ref, lam_init):
    lq = lq_ref[...]
    s01 = jnp.sum(lq[0:1] * lq[1:2], axis=-1, keepdims=True)
    s23 = jnp.sum(lq[2:3] * lq[3:4], axis=-1, keepdims=True)
    return jnp.exp(s01) - jnp.exp(s23) + lam_init


def _attn_kernel(slopes_ref, lq_ref, qT_ref, k_ref, vT_ref, g_ref, gate_ref,
                 o_ref,
                 qt_s, bias_s, s_a, s_b, p_a, p_b, mb_a, mb_b, al_a, al_b, m_s, acc_s, *, tile, lam_init):
    T = tile
    W = 2 * T
    CW = ATTN_COLS
    RC = ATTN_ROWS
    col_chunks = [slice(c * CW, (c + 1) * CW) for c in range(W // CW)]
    row_chunks = [slice(r * RC, (r + 1) * RC) for r in range(T // RC)]
    h = pl.program_id(1)
    qi = pl.program_id(2)
    slope = slopes_ref[h] * LOG2E
    shift = slope * T

    @pl.when(qi == 0)
    def _():
        for c, cols in enumerate(col_chunks):
            jl = lax.broadcasted_iota(jnp.int32, (T, CW), 0)
            il = (lax.broadcasted_iota(jnp.int32, (T, CW), 1) + c * CW) & (T - 1)
            before = slope * (jl - T).astype(F32)
            after = slope * (2 * il - jl - T).astype(F32)
            visible = (jl // CHUNK) <= (il // CHUNK)
            bias_s[0, :, cols] = before
            bias_s[1, :, cols] = jnp.where(visible, jnp.where(jl <= il, before, after), NEG_BIG)
            bias_s[2, :, cols] = jnp.full((T, CW), NEG_BIG, F32)

    qt = qT_ref[0, 0]
    row = lax.broadcasted_iota(jnp.int32, qt.shape, 0)
    zero = jnp.zeros_like(qt)
    qt_s[:, :T] = jnp.where(row < ATTN_HEAD_DIM, qt, zero)
    qt_s[:, T:] = jnp.where(row >= ATTN_HEAD_DIM, qt, zero)
    m_s[...] = jnp.full(m_s.shape, M_INIT, F32)
    acc_s[...] = jnp.zeros(acc_s.shape, F32)
    p_b[...] = jnp.zeros(p_b.shape, BF16)
    al_b[...] = jnp.ones(al_b.shape, F32)

    def stage_s(i, s_ref, mb_ref):
        blk = jnp.minimum(i, qi)
        bidx = jnp.clip(i - qi + 1, 0, 2)
        start = blk * T
        for cols in col_chunks:
            w = qt_s[:, cols]
            mx = None
            for rows in row_chunks:
                kt = k_ref[0, pl.ds(pl.multiple_of(start + rows.start, RC), RC), :]
                s = jnp.dot(kt, w, preferred_element_type=F32) + bias_s[bidx, rows, cols]
                s_ref[rows, cols] = s
                part = jnp.max(s.reshape(RC // 8, 8, CW), axis=0)
                mx = part if mx is None else jnp.maximum(mx, part)
            mb_ref[:, cols] = mx

    def stage_x(s_ref, mb_ref, p_ref, al_ref):
        for cols in col_chunks:
            m_prev = m_s[:, cols] - shift
            m_new = jnp.maximum(m_prev, jnp.max(mb_ref[:, cols], axis=0, keepdims=True))
            for rows in row_chunks:
                p_ref[rows, cols] = jnp.exp2(s_ref[rows, cols] - m_new).astype(BF16)
            al_ref[:, cols] = jnp.exp2(m_prev - m_new)
            m_s[:, cols] = m_new

    ones_rows = jnp.ones((L_ROWS, T), BF16)

    def stage_p(i, p_ref, al_ref):
        blk = jnp.clip(i, 0, qi)
        vt = vT_ref[0, 0, :, pl.ds(pl.multiple_of(blk * T, T), T)]
        vt1 = jnp.concatenate([vt, ones_rows], axis=0)
        for cols in col_chunks:
            acc_s[:, cols] = al_ref[:, cols] * acc_s[:, cols] + jnp.dot(
                vt1, p_ref[:, cols], preferred_element_type=F32)

    n_pairs = (qi + 2) // 2

    stage_s(0, s_a, mb_a)

    def body(u, carry):
        i = 2 * u
        stage_x(s_a, mb_a, p_a, al_a)
        stage_s(i + 1, s_b, mb_b)
        stage_p(i - 1, p_b, al_b)
        stage_x(s_b, mb_b, p_b, al_b)
        stage_s(i + 2, s_a, mb_a)
        stage_p(i, p_a, al_a)
        return carry

    lax.fori_loop(0, n_pairs, body, 0)
    stage_p(2 * n_pairs - 1, p_b, al_b)

    lam = _lambda_full(lq_ref, lam_init)
    accn = acc_s[0:ATTN_VAL_DIM, :] / acc_s[ATTN_VAL_DIM:ATTN_VAL_DIM + 1, :]
    oT = accn[:, :T] - lam * accn[:, T:]
    ms = jnp.mean(oT * oT, axis=0, keepdims=True)
    on = oT * lax.rsqrt(ms + EPS) * g_ref[...] * (1.0 - lam_init)
    o_ref[0] = on.T * gate_ref[0]


def _attention_prompt(slopes, lq, qT, kb, vT, sub_g_col, gate, lam_init):
    b, heads, _, s = qT.shape
    T = _row_tile(s, ATTN_TILE)
    assert T % ATTN_COLS == 0 and T % CHUNK == 0 and (T & (T - 1)) == 0
    nq = s // T
    kern = functools.partial(_attn_kernel, tile=T, lam_init=lam_init)
    return pl.pallas_call(
        kern,
        grid=(b, heads, nq),
        in_specs=[
            pl.BlockSpec(memory_space=pltpu.SMEM),
            pl.BlockSpec((4, ATTN_HEAD_DIM), lambda bi, hi, qi: (0, 0)),
            pl.BlockSpec((1, 1, 128, T), lambda bi, hi, qi: (bi, hi, 0, qi)),
            pl.BlockSpec((1, s, 128), lambda bi, hi, qi: (bi, 0, hi)),
            pl.BlockSpec((1, 1, 128, s), lambda bi, hi, qi: (bi, hi, 0, 0)),
            pl.BlockSpec((128, 1), lambda bi, hi, qi: (0, 0)),
            pl.BlockSpec((1, T, 128), lambda bi, hi, qi: (bi, qi, hi)),
        ],
        out_specs=pl.BlockSpec((1, T, 128), lambda bi, hi, qi: (bi, qi, hi)),
        out_shape=jax.ShapeDtypeStruct((b, s, heads * 128), F32),
        scratch_shapes=[
            pltpu.VMEM((128, 2 * T), BF16),
            pltpu.VMEM((3, T, 2 * T), F32),
            pltpu.VMEM((T, 2 * T), F32),
            pltpu.VMEM((T, 2 * T), F32),
            pltpu.VMEM((T, 2 * T), BF16),
            pltpu.VMEM((T, 2 * T), BF16),
            pltpu.VMEM((8, 2 * T), F32),
            pltpu.VMEM((8, 2 * T), F32),
            pltpu.VMEM((1, 2 * T), F32),
            pltpu.VMEM((1, 2 * T), F32),
            pltpu.VMEM((1, 2 * T), F32),
            pltpu.VMEM((ATTN_VAL_DIM + L_ROWS, 2 * T), F32),
        ],
        compiler_params=_cparams(("parallel", "parallel", "arbitrary")),
        name="attn_prompt",
    )(slopes, lq, qT, kb, vT, sub_g_col, gate)


def _attn_sample_kernel(slopes_ref, lq_ref, qT_ref, ck_ref, kn_ref, cv_ref, vn_ref, g_ref, gate_ref,
                        o_ref, *, past, lam_init):
    h = pl.program_id(1)
    slope = slopes_ref[h] * LOG2E
    ts = kn_ref.shape[1]
    k_all = jnp.concatenate([ck_ref[0].astype(BF16), kn_ref[0]], axis=0)
    v_all = jnp.concatenate([cv_ref[0].astype(BF16), vn_ref[0].astype(BF16)], axis=0)
    qt = qT_ref[0, 0]
    row = lax.broadcasted_iota(jnp.int32, qt.shape, 0)
    zero = jnp.zeros_like(qt)
    kpos = lax.broadcasted_iota(jnp.int32, (past + ts, ts), 0)
    qpos = past + lax.broadcasted_iota(jnp.int32, (past + ts, ts), 1)
    bias = -slope * jnp.abs(qpos - kpos).astype(F32)
    visible = (kpos // CHUNK) <= (qpos // CHUNK)

    def softmax_t(q_masked):
        s = jnp.dot(k_all, q_masked, preferred_element_type=F32)
        t = jnp.where(visible, s + bias, NEG_BIG)
        p = jnp.exp2(t - jnp.max(t, axis=0, keepdims=True))
        return p / jnp.sum(p, axis=0, keepdims=True)

    p1 = softmax_t(jnp.where(row < ATTN_HEAD_DIM, qt, zero))
    p2 = softmax_t(jnp.where(row >= ATTN_HEAD_DIM, qt, zero))
    lam = _lambda_full(lq_ref, lam_init)
    a = (p1 - lam * p2).astype(BF16)
    o = lax.dot_general(a, v_all, (((0,), (0,)), ((), ())), preferred_element_type=F32)
    ms = jnp.mean(o * o, axis=-1, keepdims=True)
    on = o * lax.rsqrt(ms + EPS) * g_ref[...] * (1.0 - lam_init)
    o_ref[0] = on * gate_ref[0]


def _attention_sample(slopes, lq, qT, cache_k, kb, cache_v, v32, sub_g_row, gate, lam_init):
    b, heads, _, ts = qT.shape
    past = cache_k.shape[1]
    kern = functools.partial(_attn_sample_kernel, past=past, lam_init=lam_init)
    col = lambda bi, hi: (bi, 0, hi)
    return pl.pallas_call(
        kern,
        grid=(b, heads),
        in_specs=[
            pl.BlockSpec(memory_space=pltpu.SMEM),
            pl.BlockSpec((4, ATTN_HEAD_DIM), lambda bi, hi: (0, 0)),
            pl.BlockSpec((1, 1, 128, ts), lambda bi, hi: (bi, hi, 0, 0)),
            pl.BlockSpec((1, past, 128), col),
            pl.BlockSpec((1, ts, 128), col),
            pl.BlockSpec((1, past, 128), col),
            pl.BlockSpec((1, ts, 128), col),
            pl.BlockSpec((1, 128), lambda bi, hi: (0, 0)),
            pl.BlockSpec((1, ts, 128), col),
        ],
        out_specs=pl.BlockSpec((1, ts, 128), col),
        out_shape=jax.ShapeDtypeStruct((b, ts, heads * 128), F32),
        compiler_params=_cparams(("parallel", "parallel")),
        name="attn_sample",
    )(slopes, lq, qT, cache_k, kb, cache_v, v32, sub_g_row, gate)


def _rec_prep_kernel(rq_ref, rk_ref, rlf_ref, rv_ref,
                     oi_ref, qe_ref, kd_ref, el_ref, lvl_s):
    C = CHUNK
    n_chunks = rq_ref.shape[1] // C
    t_idx = lax.broadcasted_iota(jnp.int32, (C, C), 0)
    s_idx = lax.broadcasted_iota(jnp.int32, (C, C), 1)
    HC = REC_HEADS * C
    t2 = lax.broadcasted_iota(jnp.int32, (HC, HC), 0)
    s2 = lax.broadcasted_iota(jnp.int32, (HC, HC), 1)
    tt = t2 % C
    ss = s2 % C
    lvl = jnp.where(tt == ss, 0, -1)
    for i, n in enumerate(LEVEL_HALVES):
        same = (tt // (2 * n)) == (ss // (2 * n))
        lvl = jnp.where(same & ((tt % (2 * n)) >= n) & ((ss % (2 * n)) < n), i + 1, lvl)
    lvl_s[...] = jnp.where((t2 // C) == (s2 // C), lvl, -1)
    tri = (s_idx <= t_idx).astype(F32).astype(BF16)

    def stack_heads(x):
        return jnp.concatenate([x[:, h * REC_DIM:(h + 1) * REC_DIM] for h in range(REC_HEADS)], axis=0)
    w = rq_ref.shape[2]
    sub8 = lax.broadcasted_iota(jnp.int32, (C // 8, 8, w), 1)
    row_odd = (lax.broadcasted_iota(jnp.int32, (C, w), 0) & 1) == 1

    def midpoint(cum, n):
        if n >= 4:
            g = cum.reshape(C // (2 * n), 2 * n, w)
            return jnp.broadcast_to(g[:, n - 1:n, :], g.shape).reshape(C, w)
        g = cum.reshape(C // 8, 8, w)
        lo = jnp.broadcast_to(g[:, 1:2, :], g.shape)
        hi = jnp.broadcast_to(g[:, 5:6, :], g.shape)
        return jnp.where(sub8 < 4, lo, hi).reshape(C, w)

    def chunk(c, carry):
        rows = pl.ds(pl.multiple_of(c * C, C), C)
        q = rq_ref[0, rows, :]
        k = rk_ref[0, rows, :]
        v = rv_ref[0, rows, :].astype(BF16)
        lf = rlf_ref[0, rows, :]
        lf_hi = lf.astype(BF16)
        rem = lf - lf_hi.astype(F32)
        lf_mid = rem.astype(BF16)
        lf_lo = (rem - lf_mid.astype(F32)).astype(BF16)
        cum = (jnp.dot(tri, lf_hi, preferred_element_type=F32)
               + jnp.dot(tri, lf_mid, preferred_element_type=F32)
               + jnp.dot(tri, lf_lo, preferred_element_type=F32))
        last = cum[C - 1:C, :]
        qe_ref[0, rows, :] = (q * jnp.exp(cum)).astype(BF16)
        kd_ref[0, rows, :] = (k * jnp.exp(last - cum)).astype(BF16)
        el_ref[0, pl.ds(c, 1), :] = jnp.exp(last)
        q_lv = [stack_heads(q.astype(BF16))]
        k_lv = [stack_heads(k.astype(BF16))]
        for n in LEVEL_HALVES:
            if n == 1:
                e = jnp.where(row_odd, jnp.exp(lf), 1.0)
            else:
                e = jnp.exp(-jnp.abs(cum - midpoint(cum, n)))
            q_lv.append(stack_heads((q * e).astype(BF16)))
            k_lv.append(stack_heads((k * e).astype(BF16)))
        a = jnp.zeros((HC, HC), F32)
        for i, (ql, kl) in enumerate(zip(q_lv, k_lv)):
            x = lax.dot_general(ql, kl, (((1,), (1,)), ((), ())), preferred_element_type=F32)
            a = jnp.where(lvl_s[...] == i, x, a)
        res = jnp.dot(a.astype(BF16), stack_heads(v), preferred_element_type=F32)
        for h in range(REC_HEADS):
            oi_ref[0, rows, h * REC_DIM:(h + 1) * REC_DIM] = res[h * C:(h + 1) * C, :]
        return carry

    lax.fori_loop(0, n_chunks, chunk, 0, unroll=2 if n_chunks % 2 == 0 else 1)


def _rec_prep(rq, rk, rlf, rv, tr):
    b, s, w = rq.shape
    nt = s // tr
    ncs = tr // CHUNK
    row = lambda bi, ti: (bi, ti, 0)
    rows_spec = pl.BlockSpec((1, tr, w), row)
    return pl.pallas_call(
        _rec_prep_kernel,
        grid=(b, nt),
        in_specs=[rows_spec] * 4,
        out_specs=[rows_spec, rows_spec, rows_spec, pl.BlockSpec((1, ncs, w), row)],
        out_shape=[jax.ShapeDtypeStruct((b, s, w), F32),
                   jax.ShapeDtypeStruct((b, s, w), BF16),
                   jax.ShapeDtypeStruct((b, s, w), BF16),
                   jax.ShapeDtypeStruct((b, s // CHUNK, w), F32)],
        scratch_shapes=[pltpu.VMEM((REC_HEADS * CHUNK, REC_HEADS * CHUNK), jnp.int32)],
        compiler_params=_cparams(("parallel", "parallel")),
        name="rec_prep",
    )(rq, rk, rlf, rv)


def _rec_seq_kernel(oi_ref, qe_ref, kd_ref, rv_ref, el_ref, gate_ref, h0_ref, g_ref,
                    o_ref, h_ref, st_s):
    C = CHUNK
    ti = pl.program_id(1)
    n_chunks = oi_ref.shape[1] // C

    @pl.when(ti == 0)
    def _():
        for h in range(REC_HEADS):
            st_s[h] = h0_ref[0, h].T

    for c in range(n_chunks):
        rows = slice(c * C, (c + 1) * C)
        for h in range(REC_HEADS):
            cols = slice(h * REC_DIM, (h + 1) * REC_DIM)
            st = st_s[h]
            o = oi_ref[0, rows, cols] + lax.dot_general(
                qe_ref[0, rows, cols], st.astype(BF16), (((1,), (1,)), ((), ())),
                preferred_element_type=F32)
            upd = lax.dot_general(rv_ref[0, rows, cols].astype(BF16), kd_ref[0, rows, cols],
                                  (((0,), (0,)), ((), ())), preferred_element_type=F32)
            st_s[h] = st * el_ref[0, c:c + 1, cols] + upd
            ms = jnp.mean(o * o, axis=-1, keepdims=True)
            on = o * lax.rsqrt(ms + EPS) * g_ref[...]
            o_ref[0, rows, cols] = on * gate_ref[0, rows, cols]

    @pl.when(ti == pl.num_programs(1) - 1)
    def _():
        for h in range(REC_HEADS):
            h_ref[0, h] = st_s[h].T


def _rec_seq(oi, qe, kd, rv, el, gate, h0, rec_g_row, tr):
    b, s, w = oi.shape
    nt = s // tr
    ncs = tr // CHUNK
    row = lambda bi, ti: (bi, ti, 0)
    rows_spec = pl.BlockSpec((1, tr, w), row)
    st_spec = pl.BlockSpec((1, REC_HEADS, REC_DIM, REC_DIM), lambda bi, ti: (bi, 0, 0, 0))
    return pl.pallas_call(
        _rec_seq_kernel,
        grid=(b, nt),
        in_specs=[rows_spec, rows_spec, rows_spec, rows_spec,
                  pl.BlockSpec((1, ncs, w), row), rows_spec, st_spec,
                  pl.BlockSpec((1, REC_DIM), lambda bi, ti: (0, 0))],
        out_specs=[rows_spec, st_spec],
        out_shape=[jax.ShapeDtypeStruct((b, s, w), F32),
                   jax.ShapeDtypeStruct((b, REC_HEADS, REC_DIM, REC_DIM), F32)],
        scratch_shapes=[pltpu.VMEM((REC_HEADS, REC_DIM, REC_DIM), F32)],
        compiler_params=_cparams(("parallel", "arbitrary")),
        name="rec_seq",
    )(oi, qe, kd, rv, el, gate, h0, rec_g_row)


def _merge_kernel(x_ref, ma_ref, mr_ref, w_ref, g_ref, y_ref):
    half = ma_ref.shape[2]
    y = x_ref[0]
    y = y + jnp.dot(ma_ref[0].astype(BF16), w_ref[:half, :], preferred_element_type=F32)
    y = y + jnp.dot(mr_ref[0].astype(BF16), w_ref[half:, :], preferred_element_type=F32)
    ms = jnp.mean(y * y, axis=-1, keepdims=True)
    y_ref[0] = y * lax.rsqrt(ms + EPS) * g_ref[...]


def _merge(x, mix_a, mix_r, w_out_bf16, final_g, tm):
    b, s, d = x.shape
    nt = s // tm
    row = lambda bi, ti: (bi, ti, 0)
    return pl.pallas_call(
        _merge_kernel,
        grid=(b, nt),
        in_specs=[
            pl.BlockSpec((1, tm, d), row),
            pl.BlockSpec((1, tm, mix_a.shape[2]), row),
            pl.BlockSpec((1, tm, mix_r.shape[2]), row),
            pl.BlockSpec(w_out_bf16.shape, lambda bi, ti: (0, 0)),
            pl.BlockSpec((1, d), lambda bi, ti: (0, 0)),
        ],
        out_specs=pl.BlockSpec((1, tm, d), row),
        out_shape=jax.ShapeDtypeStruct((b, s, d), F32),
        compiler_params=_cparams(("parallel", "parallel")),
        name="merge",
    )(x, mix_a, mix_r, w_out_bf16, final_g)


def kernel(x_prompt, x_sample, cache_k, cache_v, state_h, norm_g, w_in, lambda_qk, subln_g, rec_lb,
           rec_norm_g, w_out, final_g):
    depth = w_in.shape[0]
    assert depth == 1
    l = 0
    lam_init = 0.8 - 0.6 * math.exp(-0.3 * l)
    slopes = jnp.exp2(-8.0 * jnp.arange(1, ATTN_HEADS + 1, dtype=F32) / ATTN_HEADS)
    bp, sp, _ = x_prompt.shape
    bs, ts, _ = x_sample.shape
    past = cache_k.shape[2]

    w_in_b = w_in[l].astype(BF16)
    w_out_b = w_out[l].astype(BF16)
    g_in = norm_g[l].reshape(1, D_MODEL)
    lq = lambda_qk[l]
    sub_g = subln_g[l]
    rec_g = rec_norm_g[l].reshape(1, REC_DIM)
    fin_g = final_g.reshape(1, D_MODEL)
    lb_rows = rec_lb[l:l + 2]

    tm = _row_tile(sp, 256)
    (qT, kb, vT, k32, v32, ga, rq, rk, rlf, rv, gr) = _project(x_prompt, g_in, w_in_b, lb_rows, tm)
    mix_a = _attention_prompt(slopes, lq, qT, kb, vT, sub_g.reshape(128, 1), ga, lam_init)
    tr = _row_tile(sp, REC_TILE)
    oi, qe, kd, el = _rec_prep(rq, rk, rlf, rv, tr)
    h0 = jnp.zeros((bp, REC_HEADS, REC_DIM, REC_DIM), F32)
    mix_r, h_p = _rec_seq(oi, qe, kd, rv, el, gr, h0, rec_g, tr)
    y_prompt = _merge(x_prompt, mix_a, mix_r, w_out_b, fin_g, _row_tile(sp, 512))

    (qT_s, kb_s, _, k32_s, v32_s, ga_s, rq_s, rk_s, rlf_s, rv_s, gr_s) = _project(
        x_sample, g_in, w_in_b, lb_rows, ts)
    ck = cache_k[l].reshape(bs, past, ATTN_HEADS * 128)
    cv = cache_v[l].reshape(bs, past, ATTN_HEADS * 128)
    mix_a_s = _attention_sample(slopes, lq, qT_s, ck, kb_s, cv, v32_s, sub_g.reshape(1, 128), ga_s, lam_init)
    oi_s, qe_s, kd_s, el_s = _rec_prep(rq_s, rk_s, rlf_s, rv_s, ts)
    mix_r_s, h_s = _rec_seq(oi_s, qe_s, kd_s, rv_s, el_s, gr_s, state_h[l].astype(F32), rec_g, ts)
    y_sample = _merge(x_sample, mix_a_s, mix_r_s, w_out_b, fin_g, ts)

    shape_kv = lambda a, b_, s_: a.reshape(1, b_, s_, ATTN_HEADS, ATTN_VAL_DIM)
    return (y_prompt, y_sample,
            shape_kv(k32, bp, sp), shape_kv(v32, bp, sp), h_p[None],
            shape_kv(k32_s, bs, ts), shape_kv(v32_s, bs, ts), h_s[None])
```

```python
import functools
import math

import jax
import jax.numpy as jnp
from jax import lax
from jax.experimental import pallas as pl
from jax.experimental.pallas import tpu as pltpu

F32 = jnp.float32
BF16 = jnp.bfloat16

D_MODEL = 1024
CHUNK = 64
ATTN_HEADS = 4
ATTN_HEAD_DIM = 64
ATTN_VAL_DIM = 2 * ATTN_HEAD_DIM
REC_HEADS = 4
REC_DIM = 128
GROUP_WIDTH = 512
N_GROUPS = 8
EPS = 1e-6
LOG2E = math.log2(math.e)
ATTN_SCALE = ATTN_HEAD_DIM ** -0.5
NEG_BIG = -1e30
M_INIT = -1e20

VMEM_LIMIT_BYTES = 52 * 1024 * 1024
ATTN_TILE = 512
ATTN_COLS = 256
ATTN_ROWS = 128
L_ROWS = 16
REC_TILE = 512
LEVEL_HALVES = (32, 16, 8, 4, 2, 1)


def _row_tile(seq, want):
    return want if seq % want == 0 else seq


def _cparams(sem):
    return pltpu.CompilerParams(dimension_semantics=sem, vmem_limit_bytes=VMEM_LIMIT_BYTES)


def _sigmoid(x):
    return 1.0 / (1.0 + jnp.exp(-x))


def _proj_kernel(x_ref, g_ref, w_ref, lb_ref,
                 qT_ref, kb_ref, vT_ref, k32_ref, v32_ref, ga_ref,
                 rq_ref, rk_ref, rlf_ref, rv_ref, gr_ref):
    x = x_ref[0]
    ms = jnp.mean(x * x, axis=-1, keepdims=True)
    xn = (x * lax.rsqrt(ms + EPS) * g_ref[...]).astype(BF16)

    def proj(c):
        return jnp.dot(xn, w_ref[:, c * GROUP_WIDTH:(c + 1) * GROUP_WIDTH],
                       preferred_element_type=F32)

    qa = proj(0) * (ATTN_SCALE * LOG2E)
    for h in range(ATTN_HEADS):
        qT_ref[0, h] = qa[:, h * 128:(h + 1) * 128].T.astype(BF16)
    ka = proj(1)
    k32_ref[0] = ka
    kb_ref[0] = ka.astype(BF16)
    va = proj(2)
    v32_ref[0] = va
    for h in range(ATTN_HEADS):
        vT_ref[0, h] = va[:, h * 128:(h + 1) * 128].T.astype(BF16)
    za = proj(3)
    ga_ref[0] = za * _sigmoid(za)
    qr = proj(4)
    rq_ref[0] = qr * _sigmoid(qr)
    r0 = lb_ref[0:1, :]
    r1 = lb_ref[1:2, :]
    rmax = jnp.maximum(r0, r1)
    e0 = jnp.exp(r0 - rmax)
    e1 = jnp.exp(r1 - rmax)
    lb = e0 / (e0 + e1)
    fr = proj(5)
    key = (1.0 - lb) * _sigmoid(-fr)
    rk_ref[0] = key
    rlf_ref[0] = jnp.log1p(-key)
    rv_ref[0] = proj(6)
    zr = proj(7)
    gr_ref[0] = zr * _sigmoid(zr)


def _project(x, norm_g, w_bf16, rec_lb, tm):
    b, s, d = x.shape
    nt = s // tm
    row = lambda bi, ti: (bi, ti, 0)
    rowT = lambda bi, ti: (bi, 0, 0, ti)
    f32_rows = jax.ShapeDtypeStruct((b, s, GROUP_WIDTH), F32)
    bf_rows = jax.ShapeDtypeStruct((b, s, GROUP_WIDTH), BF16)
    bf_T = jax.ShapeDtypeStruct((b, ATTN_HEADS, 128, s), BF16)
    rows_spec = pl.BlockSpec((1, tm, GROUP_WIDTH), row)
    T_spec = pl.BlockSpec((1, ATTN_HEADS, 128, tm), rowT)
    return pl.pallas_call(
        _proj_kernel,
        grid=(b, nt),
        in_specs=[
            pl.BlockSpec((1, tm, d), row),
            pl.BlockSpec((1, d), lambda bi, ti: (0, 0)),
            pl.BlockSpec((d, N_GROUPS * GROUP_WIDTH), lambda bi, ti: (0, 0)),
            pl.BlockSpec((2, GROUP_WIDTH), lambda bi, ti: (0, 0)),
        ],
        out_specs=[T_spec, rows_spec, T_spec, rows_spec, rows_spec, rows_spec,
                   rows_spec, rows_spec, rows_spec, rows_spec, rows_spec],
        out_shape=[bf_T, bf_rows, bf_T, f32_rows, f32_rows, f32_rows,
                   f32_rows, f32_rows, f32_rows, f32_rows, f32_rows],
        compiler_params=_cparams(("parallel", "parallel")),
        name="proj",
    )(x, norm_g, w_bf16, rec_lb)


def _lambda_full(lq_ref, lam_init):
    lq = lq_ref[...]
    s01 = jnp.sum(lq[0:1] * lq[1:2], axis=-1, keepdims=True)
    s23 = jnp.sum(lq[2:3] * lq[3:4], axis=-1, keepdims=True)
    return jnp.exp(s01) - jnp.exp(s23) + lam_init


def _attn_kernel(slopes_ref, lq_ref, qT_ref, k_ref, vT_ref, g_ref, gate_ref,
                 o_ref,
                 qt_s, kaug_s, bias_s, s_a, s_b, p_a, p_b, mb_a, mb_b, al_a, al_b, m_s, acc_s,
                 *, tile, lam_init):
    T = tile
    W = 2 * T
    CW = ATTN_COLS
    RC = ATTN_ROWS
    col_chunks = [slice(c * CW, (c + 1) * CW) for c in range(W // CW)]
    row_chunks = [slice(r * RC, (r + 1) * RC) for r in range(T // RC)]
    h = pl.program_id(1)
    qi = pl.program_id(2)
    slope = slopes_ref[h] * LOG2E
    shift = slope * T

    @pl.when(qi == 0)
    def _():
        r = (lax.broadcasted_iota(jnp.int32, (T, 128), 0) - T).astype(F32)
        r_hi = r.astype(BF16).astype(F32)
        r_lo = (r - r_hi).astype(BF16).astype(F32)
        lane = lax.broadcasted_iota(jnp.int32, (T, 128), 1)
        kaug_s[...] = jnp.where(lane < 3, r_hi, jnp.where(lane < 6, r_lo, 0.0)).astype(BF16)
        sl = jnp.full((128, CW), slope, F32)
        c0 = sl.astype(BF16).astype(F32)
        c1 = (sl - c0).astype(BF16).astype(F32)
        c2 = (sl - c0 - c1).astype(BF16).astype(F32)
        rw = lax.broadcasted_iota(jnp.int32, (128, CW), 0)
        coef = jnp.where(rw % 3 == 0, c0, jnp.where(rw % 3 == 1, c1, c2))
        coef = jnp.where(rw < 6, coef, 0.0).astype(BF16)
        for c, cols in enumerate(col_chunks):
            qt_s[128:256, cols] = coef
            jl = lax.broadcasted_iota(jnp.int32, (T, CW), 0)
            il = (lax.broadcasted_iota(jnp.int32, (T, CW), 1) + c * CW) & (T - 1)
            after = slope * (2 * (il - jl)).astype(F32)
            visible = (jl // CHUNK) <= (il // CHUNK)
            bias_s[0, :, cols] = jnp.zeros((T, CW), F32)
            bias_s[1, :, cols] = jnp.where(visible, jnp.where(jl <= il, 0.0, after), NEG_BIG)
            bias_s[2, :, cols] = jnp.full((T, CW), NEG_BIG, F32)

    qt = qT_ref[0, 0]
    row = lax.broadcasted_iota(jnp.int32, qt.shape, 0)
    zero = jnp.zeros_like(qt)
    qt_s[0:128, :T] = jnp.where(row < ATTN_HEAD_DIM, qt, zero)
    qt_s[0:128, T:] = jnp.where(row >= ATTN_HEAD_DIM, qt, zero)
    m_s[...] = jnp.full(m_s.shape, M_INIT, F32)
    acc_s[...] = jnp.zeros(acc_s.shape, F32)
    p_b[...] = jnp.zeros(p_b.shape, BF16)
    al_b[...] = jnp.ones(al_b.shape, F32)

    def stage_s(i, s_ref, mb_ref, extra_bias):
        blk = jnp.minimum(i, qi)
        bidx = jnp.clip(i - qi + 1, 0, 2)
        start = blk * T
        for cols in col_chunks:
            w = qt_s[:, cols]
            mx = None
            for rows in row_chunks:
                kt = k_ref[0, pl.ds(pl.multiple_of(start + rows.start, RC), RC), :]
                kt = jnp.concatenate([kt, kaug_s[rows, :]], axis=1)
                s = jnp.dot(kt, w, preferred_element_type=F32)
                if extra_bias:
                    s = s + bias_s[bidx, rows, cols]
                s_ref[rows, cols] = s
                part = jnp.max(s.reshape(RC // 8, 8, CW), axis=0)
                mx = part if mx is None else jnp.maximum(mx, part)
            mb_ref[:, cols] = mx

    def stage_x(s_ref, mb_ref, p_ref, al_ref):
        for cols in col_chunks:
            m_prev = m_s[:, cols] - shift
            m_new = jnp.maximum(m_prev, jnp.max(mb_ref[:, cols], axis=0, keepdims=True))
            for rows in row_chunks:
                p_ref[rows, cols] = jnp.exp2(s_ref[rows, cols] - m_new).astype(BF16)
            al_ref[:, cols] = jnp.exp2(m_prev - m_new)
            m_s[:, cols] = m_new

    ones_rows = jnp.ones((L_ROWS, T), BF16)

    def stage_p(i, p_ref, al_ref):
        blk = jnp.clip(i, 0, qi)
        vt = vT_ref[0, 0, :, pl.ds(pl.multiple_of(blk * T, T), T)]
        vt1 = jnp.concatenate([vt, ones_rows], axis=0)
        for cols in col_chunks:
            acc_s[:, cols] = al_ref[:, cols] * acc_s[:, cols] + jnp.dot(
                vt1, p_ref[:, cols], preferred_element_type=F32)

    n_pairs = (qi + 2) // 2

    stage_s(0, s_a, mb_a, True)

    def trip(u, bias_first, bias_second):
        i = 2 * u
        stage_x(s_a, mb_a, p_a, al_a)
        stage_s(i + 1, s_b, mb_b, bias_first)
        stage_p(i - 1, p_b, al_b)
        stage_x(s_b, mb_b, p_b, al_b)
        if bias_second is not None:
            stage_s(i + 2, s_a, mb_a, bias_second)
        stage_p(i, p_a, al_a)

    def body(u, carry):
        trip(u, False, False)
        return carry

    lax.fori_loop(0, jnp.maximum(n_pairs - 2, 0), body, 0)

    @pl.when(n_pairs >= 2)
    def _():
        trip(n_pairs - 2, False, True)

    trip(n_pairs - 1, True, None)
    stage_p(2 * n_pairs - 1, p_b, al_b)

    lam = _lambda_full(lq_ref, lam_init)
    accn = acc_s[0:ATTN_VAL_DIM, :] / acc_s[ATTN_VAL_DIM:ATTN_VAL_DIM + 1, :]
    oT = accn[:, :T] - lam * accn[:, T:]
    ms = jnp.mean(oT * oT, axis=0, keepdims=True)
    on = oT * lax.rsqrt(ms + EPS) * g_ref[...] * (1.0 - lam_init)
    o_ref[0] = on.T * gate_ref[0]


def _attention_prompt(slopes, lq, qT, kb, vT, sub_g_col, gate, lam_init):
    b, heads, _, s = qT.shape
    T = _row_tile(s, ATTN_TILE)
    assert T % ATTN_COLS == 0 and T % CHUNK == 0 and (T & (T - 1)) == 0
    nq = s // T
    kern = functools.partial(_attn_kernel, tile=T, lam_init=lam_init)
    return pl.pallas_call(
        kern,
        grid=(b, heads, nq),
        in_specs=[
            pl.BlockSpec(memory_space=pltpu.SMEM),
            pl.BlockSpec((4, ATTN_HEAD_DIM), lambda bi, hi, qi: (0, 0)),
            pl.BlockSpec((1, 1, 128, T), lambda bi, hi, qi: (bi, hi, 0, qi)),
            pl.BlockSpec((1, s, 128), lambda bi, hi, qi: (bi, 0, hi)),
            pl.BlockSpec((1, 1, 128, s), lambda bi, hi, qi: (bi, hi, 0, 0)),
            pl.BlockSpec((128, 1), lambda bi, hi, qi: (0, 0)),
            pl.BlockSpec((1, T, 128), lambda bi, hi, qi: (bi, qi, hi)),
        ],
        out_specs=pl.BlockSpec((1, T, 128), lambda bi, hi, qi: (bi, qi, hi)),
        out_shape=jax.ShapeDtypeStruct((b, s, heads * 128), F32),
        scratch_shapes=[
            pltpu.VMEM((256, 2 * T), BF16),
            pltpu.VMEM((T, 128), BF16),
            pltpu.VMEM((3, T, 2 * T), F32),
            pltpu.VMEM((T, 2 * T), F32),
            pltpu.VMEM((T, 2 * T), F32),
            pltpu.VMEM((T, 2 * T), BF16),
            pltpu.VMEM((T, 2 * T), BF16),
            pltpu.VMEM((8, 2 * T), F32),
            pltpu.VMEM((8, 2 * T), F32),
            pltpu.VMEM((1, 2 * T), F32),
            pltpu.VMEM((1, 2 * T), F32),
            pltpu.VMEM((1, 2 * T), F32),
            pltpu.VMEM((ATTN_VAL_DIM + L_ROWS, 2 * T), F32),
        ],
        compiler_params=_cparams(("parallel", "parallel", "arbitrary")),
        name="attn_prompt",
    )(slopes, lq, qT, kb, vT, sub_g_col, gate)


def _attn_sample_kernel(slopes_ref, lq_ref, qT_ref, ck_ref, kn_ref, cv_ref, vn_ref, g_ref, gate_ref,
                        o_ref, *, past, lam_init):
    h = pl.program_id(1)
    slope = slopes_ref[h] * LOG2E
    ts = kn_ref.shape[1]
    k_all = jnp.concatenate([ck_ref[0].astype(BF16), kn_ref[0]], axis=0)
    v_all = jnp.concatenate([cv_ref[0].astype(BF16), vn_ref[0].astype(BF16)], axis=0)
    qt = qT_ref[0, 0]
    row = lax.broadcasted_iota(jnp.int32, qt.shape, 0)
    zero = jnp.zeros_like(qt)
    kpos = lax.broadcasted_iota(jnp.int32, (past + ts, ts), 0)
    qpos = past + lax.broadcasted_iota(jnp.int32, (past + ts, ts), 1)
    bias = -slope * jnp.abs(qpos - kpos).astype(F32)
    visible = (kpos // CHUNK) <= (qpos // CHUNK)

    def softmax_t(q_masked):
        s = jnp.dot(k_all, q_masked, preferred_element_type=F32)
        t = jnp.where(visible, s + bias, NEG_BIG)
        p = jnp.exp2(t - jnp.max(t, axis=0, keepdims=True))
        return p / jnp.sum(p, axis=0, keepdims=True)

    p1 = softmax_t(jnp.where(row < ATTN_HEAD_DIM, qt, zero))
    p2 = softmax_t(jnp.where(row >= ATTN_HEAD_DIM, qt, zero))
    lam = _lambda_full(lq_ref, lam_init)
    a = (p1 - lam * p2).astype(BF16)
    o = lax.dot_general(a, v_all, (((0,), (0,)), ((), ())), preferred_element_type=F32)
    ms = jnp.mean(o * o, axis=-1, keepdims=True)
    on = o * lax.rsqrt(ms + EPS) * g_ref[...] * (1.0 - lam_init)
    o_ref[0] = on * gate_ref[0]


def _attention_sample(slopes, lq, qT, cache_k, kb, cache_v, v32, sub_g_row, gate, lam_init):
    b, heads, _, ts = qT.shape
    past = cache_k.shape[1]
    kern = functools.partial(_attn_sample_kernel, past=past, lam_init=lam_init)
    col = lambda bi, hi: (bi, 0, hi)
    return pl.pallas_call(
        kern,
        grid=(b, heads),
        in_specs=[
            pl.BlockSpec(memory_space=pltpu.SMEM),
            pl.BlockSpec((4, ATTN_HEAD_DIM), lambda bi, hi: (0, 0)),
            pl.BlockSpec((1, 1, 128, ts), lambda bi, hi: (bi, hi, 0, 0)),
            pl.BlockSpec((1, past, 128), col),
            pl.BlockSpec((1, ts, 128), col),
            pl.BlockSpec((1, past, 128), col),
            pl.BlockSpec((1, ts, 128), col),
            pl.BlockSpec((1, 128), lambda bi, hi: (0, 0)),
            pl.BlockSpec((1, ts, 128), col),
        ],
        out_specs=pl.BlockSpec((1, ts, 128), col),
        out_shape=jax.ShapeDtypeStruct((b, ts, heads * 128), F32),
        compiler_params=_cparams(("parallel", "parallel")),
        name="attn_sample",
    )(slopes, lq, qT, cache_k, kb, cache_v, v32, sub_g_row, gate)


def _rec_prep_kernel(rq_ref, rk_ref, rlf_ref, rv_ref,
                     oi_ref, qe_ref, kd_ref, el_ref, lvl_s):
    C = CHUNK
    n_chunks = rq_ref.shape[1] // C
    t_idx = lax.broadcasted_iota(jnp.int32, (C, C), 0)
    s_idx = lax.broadcasted_iota(jnp.int32, (C, C), 1)
    HC = REC_HEADS * C
    t2 = lax.broadcasted_iota(jnp.int32, (HC, HC), 0)
    s2 = lax.broadcasted_iota(jnp.int32, (HC, HC), 1)
    tt = t2 % C
    ss = s2 % C
    lvl = jnp.where(tt == ss, 0, -1)
    for i, n in enumerate(LEVEL_HALVES):
        same = (tt // (2 * n)) == (ss // (2 * n))
        lvl = jnp.where(same & ((tt % (2 * n)) >= n) & ((ss % (2 * n)) < n), i + 1, lvl)
    lvl_s[...] = jnp.where((t2 // C) == (s2 // C), lvl, -1)
    tri = (s_idx <= t_idx).astype(F32).astype(BF16)

    def stack_heads(x):
        return jnp.concatenate([x[:, h * REC_DIM:(h + 1) * REC_DIM] for h in range(REC_HEADS)], axis=0)
    w = rq_ref.shape[2]
    sub8 = lax.broadcasted_iota(jnp.int32, (C // 8, 8, w), 1)
    row_odd = (lax.broadcasted_iota(jnp.int32, (C, w), 0) & 1) == 1

    def midpoint(cum, n):
        if n >= 4:
            g = cum.reshape(C // (2 * n), 2 * n, w)
            return jnp.broadcast_to(g[:, n - 1:n, :], g.shape).reshape(C, w)
        g = cum.reshape(C // 8, 8, w)
        lo = jnp.broadcast_to(g[:, 1:2, :], g.shape)
        hi = jnp.broadcast_to(g[:, 5:6, :], g.shape)
        return jnp.where(sub8 < 4, lo, hi).reshape(C, w)

    def chunk(c, carry):
        rows = pl.ds(pl.multiple_of(c * C, C), C)
        q = rq_ref[0, rows, :]
        k = rk_ref[0, rows, :]
        v = rv_ref[0, rows, :].astype(BF16)
        lf = rlf_ref[0, rows, :]
        lf_hi = lf.astype(BF16)
        rem = lf - lf_hi.astype(F32)
        lf_mid = rem.astype(BF16)
        lf_lo = (rem - lf_mid.astype(F32)).astype(BF16)
        cum = (jnp.dot(tri, lf_hi, preferred_element_type=F32)
               + jnp.dot(tri, lf_mid, preferred_element_type=F32)
               + jnp.dot(tri, lf_lo, preferred_element_type=F32))
        last = cum[C - 1:C, :]
        qe_ref[0, rows, :] = (q * jnp.exp(cum)).astype(BF16)
        kd_ref[0, rows, :] = (k * jnp.exp(last - cum)).astype(BF16)
        el_ref[0, pl.ds(c, 1), :] = jnp.exp(last)
        q_lv = [stack_heads(q.astype(BF16))]
        k_lv = [stack_heads(k.astype(BF16))]
        for n in LEVEL_HALVES:
            if n == 1:
                e = jnp.where(row_odd, jnp.exp(lf), 1.0)
            else:
                e = jnp.exp(-jnp.abs(cum - midpoint(cum, n)))
            q_lv.append(stack_heads((q * e).astype(BF16)))
            k_lv.append(stack_heads((k * e).astype(BF16)))
        a = jnp.zeros((HC, HC), F32)
        for i, (ql, kl) in enumerate(zip(q_lv, k_lv)):
            x = lax.dot_general(ql, kl, (((1,), (1,)), ((), ())), preferred_element_type=F32)
            a = jnp.where(lvl_s[...] == i, x, a)
        res = jnp.dot(a.astype(BF16), stack_heads(v), preferred_element_type=F32)
        for h in range(REC_HEADS):
            oi_ref[0, rows, h * REC_DIM:(h + 1) * REC_DIM] = res[h * C:(h + 1) * C, :]
        return carry

    lax.fori_loop(0, n_chunks, chunk, 0, unroll=2 if n_chunks % 2 == 0 else 1)


def _rec_prep(rq, rk, rlf, rv, tr):
    b, s, w = rq.shape
    nt = s // tr
    ncs = tr // CHUNK
    row = lambda bi, ti: (bi, ti, 0)
    rows_spec = pl.BlockSpec((1, tr, w), row)
    return pl.pallas_call(
        _rec_prep_kernel,
        grid=(b, nt),
        in_specs=[rows_spec] * 4,
        out_specs=[rows_spec, rows_spec, rows_spec, pl.BlockSpec((1, ncs, w), row)],
        out_shape=[jax.ShapeDtypeStruct((b, s, w), F32),
                   jax.ShapeDtypeStruct((b, s, w), BF16),
                   jax.ShapeDtypeStruct((b, s, w), BF16),
                   jax.ShapeDtypeStruct((b, s // CHUNK, w), F32)],
        scratch_shapes=[pltpu.VMEM((REC_HEADS * CHUNK, REC_HEADS * CHUNK), jnp.int32)],
        compiler_params=_cparams(("parallel", "parallel")),
        name="rec_prep",
    )(rq, rk, rlf, rv)


def _rec_seq_kernel(oi_ref, qe_ref, kd_ref, rv_ref, el_ref, gate_ref, h0_ref, g_ref,
                    o_ref, h_ref, st_s):
    C = CHUNK
    ti = pl.program_id(1)
    n_chunks = oi_ref.shape[1] // C

    @pl.when(ti == 0)
    def _():
        for h in range(REC_HEADS):
            st_s[h] = h0_ref[0, h].T

    for c in range(n_chunks):
        rows = slice(c * C, (c + 1) * C)
        for h in range(REC_HEADS):
            cols = slice(h * REC_DIM, (h + 1) * REC_DIM)
            st = st_s[h]
            o = oi_ref[0, rows, cols] + lax.dot_general(
                qe_ref[0, rows, cols], st.astype(BF16), (((1,), (1,)), ((), ())),
                preferred_element_type=F32)
            upd = lax.dot_general(rv_ref[0, rows, cols].astype(BF16), kd_ref[0, rows, cols],
                                  (((0,), (0,)), ((), ())), preferred_element_type=F32)
            st_s[h] = st * el_ref[0, c:c + 1, cols] + upd
            ms = jnp.mean(o * o, axis=-1, keepdims=True)
            on = o * lax.rsqrt(ms + EPS) * g_ref[...]
            o_ref[0, rows, cols] = on * gate_ref[0, rows, cols]

    @pl.when(ti == pl.num_programs(1) - 1)
    def _():
        for h in range(REC_HEADS):
            h_ref[0, h] = st_s[h].T


def _rec_seq(oi, qe, kd, rv, el, gate, h0, rec_g_row, tr):
    b, s, w = oi.shape
    nt = s // tr
    ncs = tr // CHUNK
    row = lambda bi, ti: (bi, ti, 0)
    rows_spec = pl.BlockSpec((1, tr, w), row)
    st_spec = pl.BlockSpec((1, REC_HEADS, REC_DIM, REC_DIM), lambda bi, ti: (bi, 0, 0, 0))
    return pl.pallas_call(
        _rec_seq_kernel,
        grid=(b, nt),
        in_specs=[rows_spec, rows_spec, rows_spec, rows_spec,
                  pl.BlockSpec((1, ncs, w), row), rows_spec, st_spec,
                  pl.BlockSpec((1, REC_DIM), lambda bi, ti: (0, 0))],
        out_specs=[rows_spec, st_spec],
        out_shape=[jax.ShapeDtypeStruct((b, s, w), F32),
                   jax.ShapeDtypeStruct((b, REC_HEADS, REC_DIM, REC_DIM), F32)],
        scratch_shapes=[pltpu.VMEM((REC_HEADS, REC_DIM, REC_DIM), F32)],
        compiler_params=_cparams(("parallel", "arbitrary")),
        name="rec_seq",
    )(oi, qe, kd, rv, el, gate, h0, rec_g_row)


def _merge_kernel(x_ref, ma_ref, mr_ref, w_ref, g_ref, y_ref):
    half = ma_ref.shape[2]
    y = x_ref[0]
    y = y + jnp.dot(ma_ref[0].astype(BF16), w_ref[:half, :], preferred_element_type=F32)
    y = y + jnp.dot(mr_ref[0].astype(BF16), w_ref[half:, :], preferred_element_type=F32)
    ms = jnp.mean(y * y, axis=-1, keepdims=True)
    y_ref[0] = y * lax.rsqrt(ms + EPS) * g_ref[...]


def _merge(x, mix_a, mix_r, w_out_bf16, final_g, tm):
    b, s, d = x.shape
    nt = s // tm
    row = lambda bi, ti: (bi, ti, 0)
    return pl.pallas_call(
        _merge_kernel,
        grid=(b, nt),
        in_specs=[
            pl.BlockSpec((1, tm, d), row),
            pl.BlockSpec((1, tm, mix_a.shape[2]), row),
            pl.BlockSpec((1, tm, mix_r.shape[2]), row),
            pl.BlockSpec(w_out_bf16.shape, lambda bi, ti: (0, 0)),
            pl.BlockSpec((1, d), lambda bi, ti: (0, 0)),
        ],
        out_specs=pl.BlockSpec((1, tm, d), row),
        out_shape=jax.ShapeDtypeStruct((b, s, d), F32),
        compiler_params=_cparams(("parallel", "parallel")),
        name="merge",
    )(x, mix_a, mix_r, w_out_bf16, final_g)


def kernel(x_prompt, x_sample, cache_k, cache_v, state_h, norm_g, w_in, lambda_qk, subln_g, rec_lb,
           rec_norm_g, w_out, final_g):
    depth = w_in.shape[0]
    assert depth == 1
    l = 0
    lam_init = 0.8 - 0.6 * math.exp(-0.3 * l)
    slopes = jnp.exp2(-8.0 * jnp.arange(1, ATTN_HEADS + 1, dtype=F32) / ATTN_HEADS)
    bp, sp, _ = x_prompt.shape
    bs, ts, _ = x_sample.shape
    past = cache_k.shape[2]

    w_in_b = w_in[l].astype(BF16)
    w_out_b = w_out[l].astype(BF16)
    g_in = norm_g[l].reshape(1, D_MODEL)
    lq = lambda_qk[l]
    sub_g = subln_g[l]
    rec_g = rec_norm_g[l].reshape(1, REC_DIM)
    fin_g = final_g.reshape(1, D_MODEL)
    lb_rows = rec_lb[l:l + 2]

    tm = _row_tile(sp, 256)
    (qT, kb, vT, k32, v32, ga, rq, rk, rlf, rv, gr) = _project(x_prompt, g_in, w_in_b, lb_rows, tm)
    mix_a = _attention_prompt(slopes, lq, qT, kb, vT, sub_g.reshape(128, 1), ga, lam_init)
    tr = _row_tile(sp, REC_TILE)
    oi, qe, kd, el = _rec_prep(rq, rk, rlf, rv, tr)
    h0 = jnp.zeros((bp, REC_HEADS, REC_DIM, REC_DIM), F32)
    mix_r, h_p = _rec_seq(oi, qe, kd, rv, el, gr, h0, rec_g, tr)
    y_prompt = _merge(x_prompt, mix_a, mix_r, w_out_b, fin_g, _row_tile(sp, 512))

    (qT_s, kb_s, _, k32_s, v32_s, ga_s, rq_s, rk_s, rlf_s, rv_s, gr_s) = _project(
        x_sample, g_in, w_in_b, lb_rows, ts)
    ck = cache_k[l].reshape(bs, past, ATTN_HEADS * 128)
    cv = cache_v[l].reshape(bs, past, ATTN_HEADS * 128)
    mix_a_s = _attention_sample(slopes, lq, qT_s, ck, kb_s, cv, v32_s, sub_g.reshape(1, 128), ga_s, lam_init)
    oi_s, qe_s, kd_s, el_s = _rec_prep(rq_s, rk_s, rlf_s, rv_s, ts)
    mix_r_s, h_s = _rec_seq(oi_s, qe_s, kd_s, rv_s, el_s, gr_s, state_h[l].astype(F32), rec_g, ts)
    y_sample = _merge(x_sample, mix_a_s, mix_r_s, w_out_b, fin_g, ts)

    shape_kv = lambda a, b_, s_: a.reshape(1, b_, s_, ATTN_HEADS, ATTN_VAL_DIM)
    return (y_prompt, y_sample,
            shape_kv(k32, bp, sp), shape_kv(v32, bp, sp), h_p[None],
            shape_kv(k32_s, bs, ts), shape_kv(v32_s, bs, ts), h_s[None])
```

```python
import functools
import math

import jax
import jax.numpy as jnp
from jax import lax
from jax.experimental import pallas as pl
from jax.experimental.pallas import tpu as pltpu

F32 = jnp.float32
BF16 = jnp.bfloat16

D_MODEL = 1024
CHUNK = 64
ATTN_HEADS = 4
ATTN_HEAD_DIM = 64
ATTN_VAL_DIM = 2 * ATTN_HEAD_DIM
REC_HEADS = 4
REC_DIM = 128
GROUP_WIDTH = 512
N_GROUPS = 8
EPS = 1e-6
LOG2E = math.log2(math.e)
ATTN_SCALE = ATTN_HEAD_DIM ** -0.5
NEG_BIG = -1e30
M_INIT = -1e20

VMEM_LIMIT_BYTES = 52 * 1024 * 1024
ATTN_TILE = 512
ATTN_COLS = 256
ATTN_ROWS = 256
L_ROWS = 16
REC_TILE = 512
LEVEL_HALVES = (32, 16, 8, 4, 2, 1)


def _row_tile(seq, want):
    return want if seq % want == 0 else seq


def _cparams(sem):
    return pltpu.CompilerParams(dimension_semantics=sem, vmem_limit_bytes=VMEM_LIMIT_BYTES)


def _sigmoid(x):
    return 1.0 / (1.0 + jnp.exp(-x))


def _proj_kernel(x_ref, g_ref, w_ref, lb_ref,
                 qT_ref, kb_ref, vT_ref, k32_ref, v32_ref, ga_ref,
                 rq_ref, rk_ref, rlf_ref, rv_ref, gr_ref):
    x = x_ref[0]
    ms = jnp.mean(x * x, axis=-1, keepdims=True)
    xn = (x * lax.rsqrt(ms + EPS) * g_ref[...]).astype(BF16)

    def proj(c):
        return jnp.dot(xn, w_ref[:, c * GROUP_WIDTH:(c + 1) * GROUP_WIDTH],
                       preferred_element_type=F32)

    qa = proj(0) * (ATTN_SCALE * LOG2E)
    for h in range(ATTN_HEADS):
        qT_ref[0, h] = qa[:, h * 128:(h + 1) * 128].T.astype(BF16)
    ka = proj(1)
    k32_ref[0] = ka
    kb_ref[0] = ka.astype(BF16)
    va = proj(2)
    v32_ref[0] = va
    for h in range(ATTN_HEADS):
        vT_ref[0, h] = va[:, h * 128:(h + 1) * 128].T.astype(BF16)
    za = proj(3)
    ga_ref[0] = za * _sigmoid(za)
    qr = proj(4)
    rq_ref[0] = qr * _sigmoid(qr)
    r0 = lb_ref[0:1, :]
    r1 = lb_ref[1:2, :]
    rmax = jnp.maximum(r0, r1)
    e0 = jnp.exp(r0 - rmax)
    e1 = jnp.exp(r1 - rmax)
    lb = e0 / (e0 + e1)
    fr = proj(5)
    key = (1.0 - lb) * _sigmoid(-fr)
    rk_ref[0] = key
    rlf_ref[0] = jnp.log1p(-key)
    rv_ref[0] = proj(6)
    zr = proj(7)
    gr_ref[0] = zr * _sigmoid(zr)


def _project(x, norm_g, w_bf16, rec_lb, tm):
    b, s, d = x.shape
    nt = s // tm
    row = lambda bi, ti: (bi, ti, 0)
    rowT = lambda bi, ti: (bi, 0, 0, ti)
    f32_rows = jax.ShapeDtypeStruct((b, s, GROUP_WIDTH), F32)
    bf_rows = jax.ShapeDtypeStruct((b, s, GROUP_WIDTH), BF16)
    bf_T = jax.ShapeDtypeStruct((b, ATTN_HEADS, 128, s), BF16)
    rows_spec = pl.BlockSpec((1, tm, GROUP_WIDTH), row)
    T_spec = pl.BlockSpec((1, ATTN_HEADS, 128, tm), rowT)
    return pl.pallas_call(
        _proj_kernel,
        grid=(b, nt),
        in_specs=[
            pl.BlockSpec((1, tm, d), row),
            pl.BlockSpec((1, d), lambda bi, ti: (0, 0)),
            pl.BlockSpec((d, N_GROUPS * GROUP_WIDTH), lambda bi, ti: (0, 0)),
            pl.BlockSpec((2, GROUP_WIDTH), lambda bi, ti: (0, 0)),
        ],
        out_specs=[T_spec, rows_spec, T_spec, rows_spec, rows_spec, rows_spec,
                   rows_spec, rows_spec, rows_spec, rows_spec, rows_spec],
        out_shape=[bf_T, bf_rows, bf_T, f32_rows, f32_rows, f32_rows,
                   f32_rows, f32_rows, f32_rows, f32_rows, f32_rows],
        compiler_params=_cparams(("parallel", "parallel")),
        name="proj",
    )(x, norm_g, w_bf16, rec_lb)


def _lambda_full(lq_ref, lam_init):
    lq = lq_ref[...]
    s01 = jnp.sum(lq[0:1] * lq[1:2], axis=-1, keepdims=True)
    s23 = jnp.sum(lq[2:3] * lq[3:4], axis=-1, keepdims=True)
    return jnp.exp(s01) - jnp.exp(s23) + lam_init


def _attn_kernel(slopes_ref, lq_ref, qT_ref, k_ref, vT_ref, g_ref, gate_ref,
                 o_ref,
                 qt_s, kaug_s, bias_s, s_a, s_b, p_a, p_b, mb_a, mb_b, al_a, al_b, m_s, acc_s,
                 *, tile, lam_init):
    T = tile
    W = 2 * T
    CW = ATTN_COLS
    RC = ATTN_ROWS
    col_chunks = [slice(c * CW, (c + 1) * CW) for c in range(W // CW)]
    row_chunks = [slice(r * RC, (r + 1) * RC) for r in range(T // RC)]
    h = pl.program_id(1)
    qi = pl.program_id(2)
    slope = slopes_ref[h] * LOG2E
    shift = slope * T

    @pl.when(qi == 0)
    def _():
        r = (lax.broadcasted_iota(jnp.int32, (T, 128), 0) - T).astype(F32)
        r_hi = r.astype(BF16).astype(F32)
        r_lo = (r - r_hi).astype(BF16).astype(F32)
        lane = lax.broadcasted_iota(jnp.int32, (T, 128), 1)
        kaug_s[...] = jnp.where(lane < 3, r_hi, jnp.where(lane < 6, r_lo, 0.0)).astype(BF16)
        sl = jnp.full((128, CW), slope, F32)
        c0 = sl.astype(BF16).astype(F32)
        c1 = (sl - c0).astype(BF16).astype(F32)
        c2 = (sl - c0 - c1).astype(BF16).astype(F32)
        rw = lax.broadcasted_iota(jnp.int32, (128, CW), 0)
        coef = jnp.where(rw % 3 == 0, c0, jnp.where(rw % 3 == 1, c1, c2))
        coef = jnp.where(rw < 6, coef, 0.0).astype(BF16)
        for c, cols in enumerate(col_chunks):
            qt_s[128:256, cols] = coef
            jl = lax.broadcasted_iota(jnp.int32, (T, CW), 0)
            il = (lax.broadcasted_iota(jnp.int32, (T, CW), 1) + c * CW) & (T - 1)
            after = slope * (2 * (il - jl)).astype(F32)
            visible = (jl // CHUNK) <= (il // CHUNK)
            bias_s[0, :, cols] = jnp.zeros((T, CW), F32)
            bias_s[1, :, cols] = jnp.where(visible, jnp.where(jl <= il, 0.0, after), NEG_BIG)
            bias_s[2, :, cols] = jnp.full((T, CW), NEG_BIG, F32)

    qt = qT_ref[0, 0]
    row = lax.broadcasted_iota(jnp.int32, qt.shape, 0)
    zero = jnp.zeros_like(qt)
    qt_s[0:128, :T] = jnp.where(row < ATTN_HEAD_DIM, qt, zero)
    qt_s[0:128, T:] = jnp.where(row >= ATTN_HEAD_DIM, qt, zero)
    m_s[...] = jnp.full(m_s.shape, M_INIT, F32)
    acc_s[...] = jnp.zeros(acc_s.shape, F32)
    p_b[...] = jnp.zeros(p_b.shape, BF16)
    al_b[...] = jnp.ones(al_b.shape, F32)

    def stage_s(i, s_ref, mb_ref, extra_bias):
        blk = jnp.minimum(i, qi)
        bidx = jnp.clip(i - qi + 1, 0, 2)
        start = blk * T
        for cols in col_chunks:
            w = qt_s[:, cols]
            mx = None
            for rows in row_chunks:
                kt = k_ref[0, pl.ds(pl.multiple_of(start + rows.start, RC), RC), :]
                kt = jnp.concatenate([kt, kaug_s[rows, :]], axis=1)
                s = jnp.dot(kt, w, preferred_element_type=F32)
                if extra_bias:
                    s = s + bias_s[bidx, rows, cols]
                s_ref[rows, cols] = s
                part = jnp.max(s.reshape(RC // 8, 8, CW), axis=0)
                mx = part if mx is None else jnp.maximum(mx, part)
            mb_ref[:, cols] = mx

    def stage_x(s_ref, mb_ref, p_ref, al_ref):
        for cols in col_chunks:
            m_prev = m_s[:, cols] - shift
            m_new = jnp.maximum(m_prev, jnp.max(mb_ref[:, cols], axis=0, keepdims=True))
            for rows in row_chunks:
                p_ref[rows, cols] = jnp.exp2(s_ref[rows, cols] - m_new).astype(BF16)
            al_ref[:, cols] = jnp.exp2(m_prev - m_new)
            m_s[:, cols] = m_new

    ones_rows = jnp.ones((L_ROWS, T), BF16)

    def stage_p(i, p_ref, al_ref):
        blk = jnp.clip(i, 0, qi)
        vt = vT_ref[0, 0, :, pl.ds(pl.multiple_of(blk * T, T), T)]
        vt1 = jnp.concatenate([vt, ones_rows], axis=0)
        for cols in col_chunks:
            acc_s[:, cols] = al_ref[:, cols] * acc_s[:, cols] + jnp.dot(
                vt1, p_ref[:, cols], preferred_element_type=F32)

    n_pairs = (qi + 2) // 2

    stage_s(0, s_a, mb_a, True)

    def trip(u, bias_first, bias_second):
        i = 2 * u
        stage_x(s_a, mb_a, p_a, al_a)
        stage_s(i + 1, s_b, mb_b, bias_first)
        stage_p(i - 1, p_b, al_b)
        stage_x(s_b, mb_b, p_b, al_b)
        if bias_second is not None:
            stage_s(i + 2, s_a, mb_a, bias_second)
        stage_p(i, p_a, al_a)

    def body(u, carry):
        trip(u, False, False)
        return carry

    lax.fori_loop(0, jnp.maximum(n_pairs - 2, 0), body, 0)

    @pl.when(n_pairs >= 2)
    def _():
        trip(n_pairs - 2, False, True)

    trip(n_pairs - 1, True, None)
    stage_p(2 * n_pairs - 1, p_b, al_b)

    lam = _lambda_full(lq_ref, lam_init)
    accn = acc_s[0:ATTN_VAL_DIM, :] / acc_s[ATTN_VAL_DIM:ATTN_VAL_DIM + 1, :]
    oT = accn[:, :T] - lam * accn[:, T:]
    ms = jnp.mean(oT * oT, axis=0, keepdims=True)
    on = oT * lax.rsqrt(ms + EPS) * g_ref[...] * (1.0 - lam_init)
    o_ref[0] = on.T * gate_ref[0]


def _attention_prompt(slopes, lq, qT, kb, vT, sub_g_col, gate, lam_init):
    b, heads, _, s = qT.shape
    T = _row_tile(s, ATTN_TILE)
    assert T % ATTN_COLS == 0 and T % CHUNK == 0 and (T & (T - 1)) == 0
    nq = s // T
    kern = functools.partial(_attn_kernel, tile=T, lam_init=lam_init)
    return pl.pallas_call(
        kern,
        grid=(b, heads, nq),
        in_specs=[
            pl.BlockSpec(memory_space=pltpu.SMEM),
            pl.BlockSpec((4, ATTN_HEAD_DIM), lambda bi, hi, qi: (0, 0)),
            pl.BlockSpec((1, 1, 128, T), lambda bi, hi, qi: (bi, hi, 0, qi)),
            pl.BlockSpec((1, s, 128), lambda bi, hi, qi: (bi, 0, hi)),
            pl.BlockSpec((1, 1, 128, s), lambda bi, hi, qi: (bi, hi, 0, 0)),
            pl.BlockSpec((128, 1), lambda bi, hi, qi: (0, 0)),
            pl.BlockSpec((1, T, 128), lambda bi, hi, qi: (bi, qi, hi)),
        ],
        out_specs=pl.BlockSpec((1, T, 128), lambda bi, hi, qi: (bi, qi, hi)),
        out_shape=jax.ShapeDtypeStruct((b, s, heads * 128), F32),
        scratch_shapes=[
            pltpu.VMEM((256, 2 * T), BF16),
            pltpu.VMEM((T, 128), BF16),
            pltpu.VMEM((3, T, 2 * T), F32),
            pltpu.VMEM((T, 2 * T), F32),
            pltpu.VMEM((T, 2 * T), F32),
            pltpu.VMEM((T, 2 * T), BF16),
            pltpu.VMEM((T, 2 * T), BF16),
            pltpu.VMEM((8, 2 * T), F32),
            pltpu.VMEM((8, 2 * T), F32),
            pltpu.VMEM((1, 2 * T), F32),
            pltpu.VMEM((1, 2 * T), F32),
            pltpu.VMEM((1, 2 * T), F32),
            pltpu.VMEM((ATTN_VAL_DIM + L_ROWS, 2 * T), F32),
        ],
        compiler_params=_cparams(("parallel", "parallel", "arbitrary")),
        name="attn_prompt",
    )(slopes, lq, qT, kb, vT, sub_g_col, gate)


def _attn_sample_kernel(slopes_ref, lq_ref, qT_ref, ck_ref, kn_ref, cv_ref, vn_ref, g_ref, gate_ref,
                        o_ref, *, past, lam_init):
    h = pl.program_id(1)
    slope = slopes_ref[h] * LOG2E
    ts = kn_ref.shape[1]
    k_all = jnp.concatenate([ck_ref[0].astype(BF16), kn_ref[0]], axis=0)
    v_all = jnp.concatenate([cv_ref[0].astype(BF16), vn_ref[0].astype(BF16)], axis=0)
    qt = qT_ref[0, 0]
    row = lax.broadcasted_iota(jnp.int32, qt.shape, 0)
    zero = jnp.zeros_like(qt)
    kpos = lax.broadcasted_iota(jnp.int32, (past + ts, ts), 0)
    qpos = past + lax.broadcasted_iota(jnp.int32, (past + ts, ts), 1)
    bias = -slope * jnp.abs(qpos - kpos).astype(F32)
    visible = (kpos // CHUNK) <= (qpos // CHUNK)

    def softmax_t(q_masked):
        s = jnp.dot(k_all, q_masked, preferred_element_type=F32)
        t = jnp.where(visible, s + bias, NEG_BIG)
        p = jnp.exp2(t - jnp.max(t, axis=0, keepdims=True))
        return p / jnp.sum(p, axis=0, keepdims=True)

    p1 = softmax_t(jnp.where(row < ATTN_HEAD_DIM, qt, zero))
    p2 = softmax_t(jnp.where(row >= ATTN_HEAD_DIM, qt, zero))
    lam = _lambda_full(lq_ref, lam_init)
    a = (p1 - lam * p2).astype(BF16)
    o = lax.dot_general(a, v_all, (((0,), (0,)), ((), ())), preferred_element_type=F32)
    ms = jnp.mean(o * o, axis=-1, keepdims=True)
    on = o * lax.rsqrt(ms + EPS) * g_ref[...] * (1.0 - lam_init)
    o_ref[0] = on * gate_ref[0]


def _attention_sample(slopes, lq, qT, cache_k, kb, cache_v, v32, sub_g_row, gate, lam_init):
    b, heads, _, ts = qT.shape
    past = cache_k.shape[1]
    kern = functools.partial(_attn_sample_kernel, past=past, lam_init=lam_init)
    col = lambda bi, hi: (bi, 0, hi)
    return pl.pallas_call(
        kern,
        grid=(b, heads),
        in_specs=[
            pl.BlockSpec(memory_space=pltpu.SMEM),
            pl.BlockSpec((4, ATTN_HEAD_DIM), lambda bi, hi: (0, 0)),
            pl.BlockSpec((1, 1, 128, ts), lambda bi, hi: (bi, hi, 0, 0)),
            pl.BlockSpec((1, past, 128), col),
            pl.BlockSpec((1, ts, 128), col),
            pl.BlockSpec((1, past, 128), col),
            pl.BlockSpec((1, ts, 128), col),
            pl.BlockSpec((1, 128), lambda bi, hi: (0, 0)),
            pl.BlockSpec((1, ts, 128), col),
        ],
        out_specs=pl.BlockSpec((1, ts, 128), col),
        out_shape=jax.ShapeDtypeStruct((b, ts, heads * 128), F32),
        compiler_params=_cparams(("parallel", "parallel")),
        name="attn_sample",
    )(slopes, lq, qT, cache_k, kb, cache_v, v32, sub_g_row, gate)


def _rec_prep_kernel(rq_ref, rk_ref, rlf_ref, rv_ref,
                     oi_ref, qe_ref, kd_ref, el_ref, lvl_s):
    C = CHUNK
    n_chunks = rq_ref.shape[1] // C
    t_idx = lax.broadcasted_iota(jnp.int32, (C, C), 0)
    s_idx = lax.broadcasted_iota(jnp.int32, (C, C), 1)
    HC = REC_HEADS * C
    t2 = lax.broadcasted_iota(jnp.int32, (HC, HC), 0)
    s2 = lax.broadcasted_iota(jnp.int32, (HC, HC), 1)
    tt = t2 % C
    ss = s2 % C
    lvl = jnp.where(tt == ss, 0, -1)
    for i, n in enumerate(LEVEL_HALVES):
        same = (tt // (2 * n)) == (ss // (2 * n))
        lvl = jnp.where(same & ((tt % (2 * n)) >= n) & ((ss % (2 * n)) < n), i + 1, lvl)
    lvl_s[...] = jnp.where((t2 // C) == (s2 // C), lvl, -1)
    tri = (s_idx <= t_idx).astype(F32).astype(BF16)

    def stack_heads(x):
        return jnp.concatenate([x[:, h * REC_DIM:(h + 1) * REC_DIM] for h in range(REC_HEADS)], axis=0)
    w = rq_ref.shape[2]
    sub8 = lax.broadcasted_iota(jnp.int32, (C // 8, 8, w), 1)
    row_odd = (lax.broadcasted_iota(jnp.int32, (C, w), 0) & 1) == 1

    def midpoint(cum, n):
        if n >= 4:
            g = cum.reshape(C // (2 * n), 2 * n, w)
            return jnp.broadcast_to(g[:, n - 1:n, :], g.shape).reshape(C, w)
        g = cum.reshape(C // 8, 8, w)
        lo = jnp.broadcast_to(g[:, 1:2, :], g.shape)
        hi = jnp.broadcast_to(g[:, 5:6, :], g.shape)
        return jnp.where(sub8 < 4, lo, hi).reshape(C, w)

    def chunk(c, carry):
        rows = pl.ds(pl.multiple_of(c * C, C), C)
        q = rq_ref[0, rows, :]
        k = rk_ref[0, rows, :]
        v = rv_ref[0, rows, :].astype(BF16)
        lf = rlf_ref[0, rows, :]
        lf_hi = lf.astype(BF16)
        rem = lf - lf_hi.astype(F32)
        lf_mid = rem.astype(BF16)
        lf_lo = (rem - lf_mid.astype(F32)).astype(BF16)
        cum = (jnp.dot(tri, lf_hi, preferred_element_type=F32)
               + jnp.dot(tri, lf_mid, preferred_element_type=F32)
               + jnp.dot(tri, lf_lo, preferred_element_type=F32))
        last = cum[C - 1:C, :]
        qe_ref[0, rows, :] = (q * jnp.exp(cum)).astype(BF16)
        kd_ref[0, rows, :] = (k * jnp.exp(last - cum)).astype(BF16)
        el_ref[0, pl.ds(c, 1), :] = jnp.exp(last)
        q_lv = [stack_heads(q.astype(BF16))]
        k_lv = [stack_heads(k.astype(BF16))]
        for n in LEVEL_HALVES:
            if n == 1:
                e = jnp.where(row_odd, jnp.exp(lf), 1.0)
            else:
                e = jnp.exp(-jnp.abs(cum - midpoint(cum, n)))
            q_lv.append(stack_heads((q * e).astype(BF16)))
            k_lv.append(stack_heads((k * e).astype(BF16)))
        a = jnp.zeros((HC, HC), F32)
        for i, (ql, kl) in enumerate(zip(q_lv, k_lv)):
            x = lax.dot_general(ql, kl, (((1,), (1,)), ((), ())), preferred_element_type=F32)
            a = jnp.where(lvl_s[...] == i, x, a)
        res = jnp.dot(a.astype(BF16), stack_heads(v), preferred_element_type=F32)
        for h in range(REC_HEADS):
            oi_ref[0, rows, h * REC_DIM:(h + 1) * REC_DIM] = res[h * C:(h + 1) * C, :]
        return carry

    lax.fori_loop(0, n_chunks, chunk, 0, unroll=4 if n_chunks % 4 == 0 else 1)


def _rec_prep(rq, rk, rlf, rv, tr):
    b, s, w = rq.shape
    nt = s // tr
    ncs = tr // CHUNK
    row = lambda bi, ti: (bi, ti, 0)
    rows_spec = pl.BlockSpec((1, tr, w), row)
    return pl.pallas_call(
        _rec_prep_kernel,
        grid=(b, nt),
        in_specs=[rows_spec] * 4,
        out_specs=[rows_spec, rows_spec, rows_spec, pl.BlockSpec((1, ncs, w), row)],
        out_shape=[jax.ShapeDtypeStruct((b, s, w), F32),
                   jax.ShapeDtypeStruct((b, s, w), BF16),
                   jax.ShapeDtypeStruct((b, s, w), BF16),
                   jax.ShapeDtypeStruct((b, s // CHUNK, w), F32)],
        scratch_shapes=[pltpu.VMEM((REC_HEADS * CHUNK, REC_HEADS * CHUNK), jnp.int32)],
        compiler_params=_cparams(("parallel", "parallel")),
        name="rec_prep",
    )(rq, rk, rlf, rv)


def _rec_seq_kernel(oi_ref, qe_ref, kd_ref, rv_ref, el_ref, gate_ref, h0_ref, g_ref,
                    o_ref, h_ref, st_s):
    C = CHUNK
    ti = pl.program_id(1)
    n_chunks = oi_ref.shape[1] // C

    @pl.when(ti == 0)
    def _():
        for h in range(REC_HEADS):
            st_s[h] = h0_ref[0, h].T

    for c in range(n_chunks):
        rows = slice(c * C, (c + 1) * C)
        for h in range(REC_HEADS):
            cols = slice(h * REC_DIM, (h + 1) * REC_DIM)
            st = st_s[h]
            o = oi_ref[0, rows, cols] + lax.dot_general(
                qe_ref[0, rows, cols], st.astype(BF16), (((1,), (1,)), ((), ())),
                preferred_element_type=F32)
            upd = lax.dot_general(rv_ref[0, rows, cols].astype(BF16), kd_ref[0, rows, cols],
                                  (((0,), (0,)), ((), ())), preferred_element_type=F32)
            st_s[h] = st * el_ref[0, c:c + 1, cols] + upd
            ms = jnp.mean(o * o, axis=-1, keepdims=True)
            on = o * lax.rsqrt(ms + EPS) * g_ref[...]
            o_ref[0, rows, cols] = on * gate_ref[0, rows, cols]

    @pl.when(ti == pl.num_programs(1) - 1)
    def _():
        for h in range(REC_HEADS):
            h_ref[0, h] = st_s[h].T


def _rec_seq(oi, qe, kd, rv, el, gate, h0, rec_g_row, tr):
    b, s, w = oi.shape
    nt = s // tr
    ncs = tr // CHUNK
    row = lambda bi, ti: (bi, ti, 0)
    rows_spec = pl.BlockSpec((1, tr, w), row)
    st_spec = pl.BlockSpec((1, REC_HEADS, REC_DIM, REC_DIM), lambda bi, ti: (bi, 0, 0, 0))
    return pl.pallas_call(
        _rec_seq_kernel,
        grid=(b, nt),
        in_specs=[rows_spec, rows_spec, rows_spec, rows_spec,
                  pl.BlockSpec((1, ncs, w), row), rows_spec, st_spec,
                  pl.BlockSpec((1, REC_DIM), lambda bi, ti: (0, 0))],
        out_specs=[rows_spec, st_spec],
        out_shape=[jax.ShapeDtypeStruct((b, s, w), F32),
                   jax.ShapeDtypeStruct((b, REC_HEADS, REC_DIM, REC_DIM), F32)],
        scratch_shapes=[pltpu.VMEM((REC_HEADS, REC_DIM, REC_DIM), F32)],
        compiler_params=_cparams(("parallel", "arbitrary")),
        name="rec_seq",
    )(oi, qe, kd, rv, el, gate, h0, rec_g_row)


def _merge_kernel(x_ref, ma_ref, mr_ref, w_ref, g_ref, y_ref):
    half = ma_ref.shape[2]
    y = x_ref[0]
    y = y + jnp.dot(ma_ref[0].astype(BF16), w_ref[:half, :], preferred_element_type=F32)
    y = y + jnp.dot(mr_ref[0].astype(BF16), w_ref[half:, :], preferred_element_type=F32)
    ms = jnp.mean(y * y, axis=-1, keepdims=True)
    y_ref[0] = y * lax.rsqrt(ms + EPS) * g_ref[...]


def _merge(x, mix_a, mix_r, w_out_bf16, final_g, tm):
    b, s, d = x.shape
    nt = s // tm
    row = lambda bi, ti: (bi, ti, 0)
    return pl.pallas_call(
        _merge_kernel,
        grid=(b, nt),
        in_specs=[
            pl.BlockSpec((1, tm, d), row),
            pl.BlockSpec((1, tm, mix_a.shape[2]), row),
            pl.BlockSpec((1, tm, mix_r.shape[2]), row),
            pl.BlockSpec(w_out_bf16.shape, lambda bi, ti: (0, 0)),
            pl.BlockSpec((1, d), lambda bi, ti: (0, 0)),
        ],
        out_specs=pl.BlockSpec((1, tm, d), row),
        out_shape=jax.ShapeDtypeStruct((b, s, d), F32),
        compiler_params=_cparams(("parallel", "parallel")),
        name="merge",
    )(x, mix_a, mix_r, w_out_bf16, final_g)


def kernel(x_prompt, x_sample, cache_k, cache_v, state_h, norm_g, w_in, lambda_qk, subln_g, rec_lb,
           rec_norm_g, w_out, final_g):
    depth = w_in.shape[0]
    assert depth == 1
    l = 0
    lam_init = 0.8 - 0.6 * math.exp(-0.3 * l)
    slopes = jnp.exp2(-8.0 * jnp.arange(1, ATTN_HEADS + 1, dtype=F32) / ATTN_HEADS)
    bp, sp, _ = x_prompt.shape
    bs, ts, _ = x_sample.shape
    past = cache_k.shape[2]

    w_in_b = w_in[l].astype(BF16)
    w_out_b = w_out[l].astype(BF16)
    g_in = norm_g[l].reshape(1, D_MODEL)
    lq = lambda_qk[l]
    sub_g = subln_g[l]
    rec_g = rec_norm_g[l].reshape(1, REC_DIM)
    fin_g = final_g.reshape(1, D_MODEL)
    lb_rows = rec_lb[l:l + 2]

    tm = _row_tile(sp, 256)
    (qT, kb, vT, k32, v32, ga, rq, rk, rlf, rv, gr) = _project(x_prompt, g_in, w_in_b, lb_rows, tm)
    mix_a = _attention_prompt(slopes, lq, qT, kb, vT, sub_g.reshape(128, 1), ga, lam_init)
    tr = _row_tile(sp, REC_TILE)
    oi, qe, kd, el = _rec_prep(rq, rk, rlf, rv, tr)
    h0 = jnp.zeros((bp, REC_HEADS, REC_DIM, REC_DIM), F32)
    mix_r, h_p = _rec_seq(oi, qe, kd, rv, el, gr, h0, rec_g, tr)
    y_prompt = _merge(x_prompt, mix_a, mix_r, w_out_b, fin_g, _row_tile(sp, 512))

    (qT_s, kb_s, _, k32_s, v32_s, ga_s, rq_s, rk_s, rlf_s, rv_s, gr_s) = _project(
        x_sample, g_in, w_in_b, lb_rows, ts)
    ck = cache_k[l].reshape(bs, past, ATTN_HEADS * 128)
    cv = cache_v[l].reshape(bs, past, ATTN_HEADS * 128)
    mix_a_s = _attention_sample(slopes, lq, qT_s, ck, kb_s, cv, v32_s, sub_g.reshape(1, 128), ga_s, lam_init)
    oi_s, qe_s, kd_s, el_s = _rec_prep(rq_s, rk_s, rlf_s, rv_s, ts)
    mix_r_s, h_s = _rec_seq(oi_s, qe_s, kd_s, rv_s, el_s, gr_s, state_h[l].astype(F32), rec_g, ts)
    y_sample = _merge(x_sample, mix_a_s, mix_r_s, w_out_b, fin_g, ts)

    shape_kv = lambda a, b_, s_: a.reshape(1, b_, s_, ATTN_HEADS, ATTN_VAL_DIM)
    return (y_prompt, y_sample,
            shape_kv(k32, bp, sp), shape_kv(v32, bp, sp), h_p[None],
            shape_kv(k32_s, bs, ts), shape_kv(v32_s, bs, ts), h_s[None])
```
